```python
import math
import jax
import jax.numpy as jnp
from jax import lax
import numpy as np

D_MODEL = 1024
BATCH = 8
SEQ = 4096
DEPTH = 2
DEC_BATCH = 32
DEC_SEQ = 1
PAST_LEN = 16384
PAGE_SIZE = 128

DN_H = 4
DN_DK = 128
DN_DV = 128
DN_CONV_CH = DN_H * (2 * DN_DK + DN_DV)
CONV_W = 4
DN_CHUNK = 64
DA_H = 4
DA_DH = 64
Q_BLOCK = 128
ROPE_THETA = 10000.0
D_FF = 2816
N_EXPERTS = 8
TOP_K = 2
D_FF_EXPERT = 1408
N_DENSE = (DEPTH + 1) // 2
N_MOE = DEPTH // 2
EPS = 1e-6
IN_SIZES = (DN_CONV_CH, DN_H * DN_DV, DN_H, DN_H, 2 * DA_H * DA_DH, 2 * DA_H * DA_DH, DA_H * 2 * DA_DH, 2 * D_MODEL)
IN_DIM = sum(IN_SIZES)

kernel_name = 'hybrid_deltanet_diffattn_decoder_step'


def _split_points():
    pts, acc = [], 0
    for s in IN_SIZES[:-1]:
        acc += s
        pts.append(acc)
    return pts


def rms_norm(x, g):
    xf = x.astype(jnp.float32)
    y = xf * lax.rsqrt(jnp.mean(xf * xf, axis=-1, keepdims=True) + EPS)
    return (y * g.astype(jnp.float32)).astype(x.dtype)


def l2_norm(x):
    xf = x.astype(jnp.float32)
    return xf * lax.rsqrt(jnp.sum(xf * xf, axis=-1, keepdims=True) + EPS)


def rope(x, pos):
    half = x.shape[-1] // 2
    inv = ROPE_THETA ** (-jnp.arange(half, dtype=jnp.float32) / half)
    ang = pos.astype(jnp.float32)[:, None] * inv[None, :]
    cos = jnp.cos(ang)[None, :, None, :]
    sin = jnp.sin(ang)[None, :, None, :]
    xf = x.astype(jnp.float32)
    x1, x2 = xf[..., :half], xf[..., half:]
    return jnp.concatenate([x1 * cos - x2 * sin, x2 * cos + x1 * sin], axis=-1).astype(x.dtype)


def gated_delta_rule(q, k, v, g, beta, s0):
    B, T, H, _ = q.shape
    dv = v.shape[-1]
    C = min(DN_CHUNK, T)
    n = -(-T // C)
    pad = n * C - T

    def prep(a):
        a = a.astype(jnp.float32)
        a = jnp.pad(a, [(0, 0), (0, pad)] + [(0, 0)] * (a.ndim - 2))
        a = a.reshape((B, n, C) + a.shape[2:])
        return jnp.swapaxes(jnp.moveaxis(a, 1, 0), 2, 3)

    q, k, v, g, beta = prep(q), prep(k), prep(v), prep(g), prep(beta)
    gc = jnp.cumsum(g, axis=-1)
    idx = jnp.arange(C)
    causal = idx[:, None] >= idx[None, :]
    strict = idx[:, None] > idx[None, :]
    decay = jnp.exp(jnp.where(causal, gc[..., :, None] - gc[..., None, :], -jnp.inf))
    kb = k * beta[..., None]
    vb = v * beta[..., None]
    low = jnp.where(strict, jnp.einsum('nbhid,nbhjd->nbhij', kb, k) * decay, 0.0)
    a_mat = low + jnp.eye(C, dtype=jnp.float32)
    rhs = jnp.concatenate([vb, kb * jnp.exp(gc)[..., None]], axis=-1)
    sol = lax.linalg.triangular_solve(a_mat, rhs, left_side=True, lower=True, unit_diagonal=True)
    u, w = sol[..., :dv], sol[..., dv:]
    qk = jnp.einsum('nbhid,nbhjd->nbhij', q, k) * decay
    q_dec = q * jnp.exp(gc)[..., None]
    k_dec = k * jnp.exp(gc[..., -1:] - gc)[..., None]
    g_last = jnp.exp(gc[..., -1])

    def step(s, xs):
        u_c, w_c, qk_c, qd_c, kd_c, gl_c = xs
        v_new = u_c - jnp.einsum('bhck,bhkv->bhcv', w_c, s)
        o = jnp.einsum('bhck,bhkv->bhcv', qd_c, s) + jnp.einsum('bhij,bhjv->bhiv', qk_c, v_new)
        s = s * gl_c[..., None, None] + jnp.einsum('bhck,bhcv->bhkv', kd_c, v_new)
        return s, o

    s, o = lax.scan(step, s0.astype(jnp.float32), (u, w, qk, q_dec, k_dec, g_last))
    o = jnp.moveaxis(jnp.swapaxes(o, 2, 3), 0, 1).reshape(B, n * C, H, dv)[:, :T]
    return o, s


def deltanet_branch(qkv_pre, z, b_raw, a_raw, conv_buf, s0, conv_w, a_log, dt_bias, norm_g):
    B, T, _ = qkv_pre.shape
    xc = jnp.concatenate([conv_buf.astype(qkv_pre.dtype), qkv_pre], axis=1)
    acc = xc[:, 0:T] * conv_w[0]
    for j in range(1, CONV_W):
        acc = acc + xc[:, j:j + T] * conv_w[j]
    new_buf = xc[:, xc.shape[1] - (CONV_W - 1):]
    act = jax.nn.silu(acc)
    q, k, v = jnp.split(act, [DN_H * DN_DK, 2 * DN_H * DN_DK], axis=-1)
    q = l2_norm(q.reshape(B, T, DN_H, DN_DK)) * (DN_DK ** -0.5)
    k = l2_norm(k.reshape(B, T, DN_H, DN_DK))
    v = v.reshape(B, T, DN_H, DN_DV)
    beta = jax.nn.sigmoid(b_raw.astype(jnp.float32))
    g = -jnp.exp(a_log.astype(jnp.float32)) * jax.nn.softplus(a_raw.astype(jnp.float32) + dt_bias.astype(jnp.float32))
    o, s = gated_delta_rule(q, k, v, g, beta, s0)
    o = rms_norm(o, norm_g) * jax.nn.silu(z.reshape(B, T, DN_H, DN_DV).astype(jnp.float32))
    return o.reshape(B, T, DN_H * DN_DV).astype(qkv_pre.dtype), new_buf, s


def diff_attend(q, k, v, q_pos, k_pos, lam):
    B, Tq = q.shape[0], q.shape[1]
    bq = min(Q_BLOCK, Tq)
    nb = -(-Tq // bq)
    pad = nb * bq - Tq
    qb = jnp.moveaxis(jnp.pad(q, ((0, 0), (0, pad), (0, 0), (0, 0))).reshape(B, nb, bq, 2 * DA_H, DA_DH), 1, 0)
    pb = jnp.pad(q_pos, (0, pad), mode='edge').reshape(nb, bq)
    scale = DA_DH ** -0.5

    def block(args):
        qi, pi = args
        s = jnp.einsum('bqmd,bkmd->bmqk', qi, k).astype(jnp.float32) * scale
        s = jnp.where((k_pos[None, :] <= pi[:, None])[None, None], s, -jnp.inf)
        pr = jax.nn.softmax(s, axis=-1).reshape(B, DA_H, 2, bq, -1)
        d = (pr[:, :, 0] - lam * pr[:, :, 1]).astype(v.dtype)
        return jnp.einsum('bhqk,bkhe->bqhe', d, v)

    o = lax.map(block, (qb, pb))
    return jnp.moveaxis(o, 0, 1).reshape(B, nb * bq, DA_H, 2 * DA_DH)[:, :Tq]


def diff_attn_branch(q_raw, k_raw, v_raw, q_pos, k_past, v_past, qn_g, kn_g, lq1, lk1, lq2, lk2, subln_g, lam_init):
    B, T, _ = q_raw.shape
    q = rope(rms_norm(q_raw.reshape(B, T, 2 * DA_H, DA_DH), qn_g), q_pos)
    k = rope(rms_norm(k_raw.reshape(B, T, 2 * DA_H, DA_DH), kn_g), q_pos)
    v = v_raw.reshape(B, T, DA_H, 2 * DA_DH)
    if k_past is None:
        k_all, v_all, k_pos = k, v, q_pos
    else:
        k_all = jnp.concatenate([k_past.astype(k.dtype), k], axis=1)
        v_all = jnp.concatenate([v_past.astype(v.dtype), v], axis=1)
        k_pos = jnp.arange(k_all.shape[1], dtype=jnp.int32)
    f32 = jnp.float32
    lam = (jnp.exp(jnp.sum(lq1.astype(f32) * lk1.astype(f32))) - jnp.exp(jnp.sum(lq2.astype(f32) * lk2.astype(f32))) + lam_init)
    o = diff_attend(q, k_all, v_all, q_pos, k_pos, lam)
    o = rms_norm(o, subln_g) * (1.0 - lam_init)
    return o.reshape(B, T, DA_H * 2 * DA_DH), k, v


def swiglu(h, w_gu, w_down):
    gu = jnp.einsum('btd,df->btf', h, w_gu)
    g, u = jnp.split(gu, 2, axis=-1)
    return jnp.einsum('btf,fd->btd', jax.nn.silu(g) * u, w_down)


def moe_ffn(h, w_router, w_gu, w_down):
    logits = jnp.einsum('btd,de->bte', h, w_router).astype(jnp.float32)
    top_v, top_i = lax.top_k(logits, TOP_K)
    top_w = jax.nn.softmax(top_v, axis=-1)
    comb = jnp.sum(jax.nn.one_hot(top_i, N_EXPERTS, dtype=jnp.float32) * top_w[..., None], axis=-2).astype(h.dtype)
    out = comb[..., 0:1] * swiglu(h, w_gu[0], w_down[0])
    for e in range(1, N_EXPERTS):
        out = out + comb[..., e:e + 1] * swiglu(h, w_gu[e], w_down[e])
    return out


def run_trunk(x, pos, conv0, ssm0, k_pool, v_pool, page_table, p):
    splits = _split_points()
    k_rows, v_rows, ssm_out, conv_out = [], [], [], []
    for l in range(DEPTH):
        lam_init = 0.8 - 0.6 * math.exp(-0.3 * l)
        h = rms_norm(x, p['attn_norm'][l])
        proj = jnp.einsum('btd,de->bte', h, p['w_in'][l])
        qkv_pre, z, b_raw, a_raw, q_raw, k_raw, v_raw, g_raw = jnp.split(proj, splits, axis=-1)
        a_out, new_buf, s = deltanet_branch(qkv_pre, z, b_raw, a_raw, conv0[l], ssm0[l], p['conv_w'][l],
                                            p['a_log'][l], p['dt_bias'][l], p['dn_norm'][l])
        if k_pool is None:
            k_past, v_past = None, None
        else:
            nseq = page_table.shape[0]
            k_past = k_pool[l, page_table].reshape(nseq, -1, 2 * DA_H, DA_DH)
            v_past = v_pool[l, page_table].reshape(nseq, -1, DA_H, 2 * DA_DH)
        b_out, k_new, v_new = diff_attn_branch(q_raw, k_raw, v_raw, pos, k_past, v_past, p['q_norm'][l], p['k_norm'][l],
                                               p['lambda_q1'][l], p['lambda_k1'][l], p['lambda_q2'][l], p['lambda_k2'][l],
                                               p['subln'][l], lam_init)
        gate_a, gate_b = jnp.split(jax.nn.sigmoid(g_raw), 2, axis=-1)
        mixed = (gate_a * jnp.einsum('bte,ed->btd', a_out, p['w_a_proj'][l])
                 + gate_b * jnp.einsum('bte,ed->btd', b_out, p['w_b_proj'][l]))
        x = x + jnp.einsum('btd,de->bte', mixed, p['w_out'][l])
        h = rms_norm(x, p['ffn_norm'][l])
        if l % 2 == 0:
            x = x + swiglu(h, p['w_ffn_gu'][l // 2], p['w_ffn_down'][l // 2])
        else:
            x = x + moe_ffn(h, p['w_router'][l // 2], p['w_exp_gu'][l // 2], p['w_exp_down'][l // 2])
        k_rows.append(k_new)
        v_rows.append(v_new)
        ssm_out.append(s.astype(ssm0.dtype))
        conv_out.append(new_buf)
    return x, jnp.stack(k_rows), jnp.stack(v_rows), jnp.stack(ssm_out), jnp.stack(conv_out)


def setup_inputs(seed: int = 0) -> dict:
    key = jax.random.key(seed)
    ks = jax.random.split(key, 32)
    f32 = jnp.float32
    n_pages = PAST_LEN // PAGE_SIZE
    n_pool = (DEC_BATCH * n_pages * 5) // 4
    nrm = lambda k, shape, s=1.0: jax.random.normal(k, shape, f32) * s
    gain = lambda k, shape: 1.0 + 0.02 * jax.random.normal(k, shape, f32)
    page_table = jax.random.permutation(ks[6], n_pool)[:DEC_BATCH * n_pages].reshape(DEC_BATCH, n_pages).astype(jnp.int32)
    return {
        'x_prompt': nrm(ks[0], (BATCH, SEQ, D_MODEL)),
        'x_sample': nrm(ks[1], (DEC_BATCH, DEC_SEQ, D_MODEL)),
        'cache_k': nrm(ks[2], (DEPTH, n_pool, PAGE_SIZE, 2 * DA_H, DA_DH)),
        'cache_v': nrm(ks[3], (DEPTH, n_pool, PAGE_SIZE, DA_H, 2 * DA_DH)),
        'state_ssm': nrm(ks[4], (DEPTH, DEC_BATCH, DN_H, DN_DK, DN_DV), DN_DK ** -0.5),
        'state_conv': nrm(ks[5], (DEPTH, DEC_BATCH, CONV_W - 1, DN_CONV_CH)),
        'page_table': page_table,
        'attn_norm': gain(ks[7], (DEPTH, D_MODEL)),
        'w_in': nrm(ks[8], (DEPTH, D_MODEL, IN_DIM), D_MODEL ** -0.5),
        'conv_w': nrm(ks[9], (DEPTH, CONV_W, DN_CONV_CH), CONV_W ** -0.5),
        'a_log': jnp.log(jax.random.uniform(ks[10], (DEPTH, DN_H), f32, 1.0, 16.0)),
        'dt_bias': nrm(ks[11], (DEPTH, DN_H), 0.1),
        'dn_norm': gain(ks[12], (DEPTH, DN_DV)),
        'q_norm': gain(ks[13], (DEPTH, DA_DH)),
        'k_norm': gain(ks[14], (DEPTH, DA_DH)),
        'lambda_q1': nrm(ks[15], (DEPTH, DA_DH), 0.1),
        'lambda_k1': nrm(ks[16], (DEPTH, DA_DH), 0.1),
        'lambda_q2': nrm(ks[17], (DEPTH, DA_DH), 0.1),
        'lambda_k2': nrm(ks[18], (DEPTH, DA_DH), 0.1),
        'subln': gain(ks[19], (DEPTH, 2 * DA_DH)),
        'w_a_proj': nrm(ks[20], (DEPTH, DN_H * DN_DV, D_MODEL), (DN_H * DN_DV) ** -0.5),
        'w_b_proj': nrm(ks[21], (DEPTH, DA_H * 2 * DA_DH, D_MODEL), (DA_H * 2 * DA_DH) ** -0.5),
        'w_out': nrm(ks[22], (DEPTH, D_MODEL, D_MODEL), D_MODEL ** -0.5),
        'ffn_norm': gain(ks[23], (DEPTH, D_MODEL)),
        'w_ffn_gu': nrm(ks[24], (N_DENSE, D_MODEL, 2 * D_FF), D_MODEL ** -0.5),
        'w_ffn_down': nrm(ks[25], (N_DENSE, D_FF, D_MODEL), D_FF ** -0.5),
        'w_router': nrm(ks[26], (N_MOE, D_MODEL, N_EXPERTS), D_MODEL ** -0.5),
        'w_exp_gu': nrm(ks[27], (N_MOE, N_EXPERTS, D_MODEL, 2 * D_FF_EXPERT), D_MODEL ** -0.5),
        'w_exp_down': nrm(ks[28], (N_MOE, N_EXPERTS, D_FF_EXPERT, D_MODEL), D_FF_EXPERT ** -0.5),
    }


def reference(x_prompt, x_sample, cache_k, cache_v, state_ssm, state_conv, page_table, attn_norm, w_in, conv_w,
              a_log, dt_bias, dn_norm, q_norm, k_norm, lambda_q1, lambda_k1, lambda_q2, lambda_k2, subln,
              w_a_proj, w_b_proj, w_out, ffn_norm, w_ffn_gu, w_ffn_down, w_router, w_exp_gu, w_exp_down):
    p = dict(attn_norm=attn_norm, w_in=w_in, conv_w=conv_w, a_log=a_log, dt_bias=dt_bias, dn_norm=dn_norm,
             q_norm=q_norm, k_norm=k_norm, lambda_q1=lambda_q1, lambda_k1=lambda_k1, lambda_q2=lambda_q2,
             lambda_k2=lambda_k2, subln=subln, w_a_proj=w_a_proj, w_b_proj=w_b_proj, w_out=w_out,
             ffn_norm=ffn_norm, w_ffn_gu=w_ffn_gu, w_ffn_down=w_ffn_down, w_router=w_router,
             w_exp_gu=w_exp_gu, w_exp_down=w_exp_down)
    bp, tp = x_prompt.shape[0], x_prompt.shape[1]
    conv0 = jnp.zeros((DEPTH, bp, CONV_W - 1, DN_CONV_CH), x_prompt.dtype)
    ssm0 = jnp.zeros((DEPTH, bp, DN_H, DN_DK, DN_DV), state_ssm.dtype)
    pos_prompt = jnp.arange(tp, dtype=jnp.int32)
    past_len = page_table.shape[1] * PAGE_SIZE
    pos_sample = past_len + jnp.arange(x_sample.shape[1], dtype=jnp.int32)
    y_prompt, k_prompt, v_prompt, ssm_prompt, conv_prompt = run_trunk(
        x_prompt, pos_prompt, conv0, ssm0, None, None, None, p)
    y_sample, k_sample, v_sample, ssm_sample, conv_sample = run_trunk(
        x_sample, pos_sample, state_conv, state_ssm, cache_k, cache_v, page_table, p)
    return (y_prompt, y_sample, k_prompt, v_prompt, ssm_prompt, conv_prompt, k_sample, v_sample, ssm_sample, conv_sample)
```

```python
import functools
import math

import jax
import jax.numpy as jnp
from jax import lax
from jax.experimental import pallas as pl
from jax.experimental.pallas import tpu as pltpu

F32 = jnp.float32
BF16 = jnp.bfloat16
EPS = 1e-6
LANES = 128
VMEM_LIMIT = 56 * 1024 * 1024

DN_H, DN_DK, DN_DV = 4, 128, 128
DN_QK = DN_H * DN_DK
DN_CH = 3 * DN_QK
CONV_W = 4
DN_CHUNK = 64
DA_H, DA_DH = 4, 64
DA_W = 2 * DA_H * DA_DH
ROPE_THETA = 10000.0
PAGE = 128

_NN = (((1,), (0,)), ((), ()))
_NT = (((1,), (1,)), ((), ()))


def _cparams(sem):
    return pltpu.CompilerParams(dimension_semantics=sem, vmem_limit_bytes=VMEM_LIMIT)


def _mm(a, b, dims=_NN):
    return lax.dot_general(a, b, dims, preferred_element_type=F32)


def _split(a):
    if a.dtype == BF16:
        return a, None
    hi = a.astype(BF16)
    lo = (a - hi.astype(F32)).astype(BF16)
    return hi, lo


def _dot(a, b, hp, dims=_NN):
    if not hp:
        return _mm(a.astype(BF16), b.astype(BF16), dims)
    ah, al = _split(a)
    bh, bl = _split(b)
    out = _mm(ah, bh, dims)
    if al is not None:
        out = out + _mm(al, bh, dims)
    if bl is not None:
        out = out + _mm(ah, bl, dims)
    return out


def _split3(a):
    hi = a.astype(BF16)
    r = a - hi.astype(F32)
    mid = r.astype(BF16)
    lo = (r - mid.astype(F32)).astype(BF16)
    return hi, mid, lo


def _dot_exact_lhs(a_bf16, b):
    hi, mid, lo = _split3(b)
    return _mm(a_bf16, hi) + (_mm(a_bf16, mid) + _mm(a_bf16, lo))


def _dot_exact_rhs(a, b_bf16):
    hi, mid, lo = _split3(a)
    return _mm(hi, b_bf16) + (_mm(mid, b_bf16) + _mm(lo, b_bf16))


def _rms(x, g):
    return x * lax.rsqrt(jnp.mean(x * x, axis=-1, keepdims=True) + EPS) * g


def _silu(x):
    return x * jax.nn.sigmoid(x)


def _norm_proj_kernel(x_ref, g_ref, *refs, n_w, hp):
    w_refs, o_refs = refs[:n_w], refs[n_w:]
    h = _rms(x_ref[...], g_ref[...])
    if not hp:
        h = h.astype(BF16)
    for w_ref, o_ref in zip(w_refs, o_refs):
        o_ref[...] = _dot(h, w_ref[...], hp).astype(o_ref.dtype)


def _norm_proj(x, g, ws, *, hp, tm):
    n, d = x.shape
    assert n % tm == 0
    in_specs = [pl.BlockSpec((tm, d), lambda i: (i, 0)), pl.BlockSpec((1, d), lambda i: (0, 0))]
    in_specs += [pl.BlockSpec(w.shape, lambda i: (0, 0)) for w in ws]
    out_specs = [pl.BlockSpec((tm, w.shape[1]), lambda i: (i, 0)) for w in ws]
    out_shape = [jax.ShapeDtypeStruct((n, w.shape[1]), F32) for w in ws]
    return pl.pallas_call(
        functools.partial(_norm_proj_kernel, n_w=len(ws), hp=hp),
        grid=(n // tm,), in_specs=in_specs, out_specs=out_specs, out_shape=out_shape,
        compiler_params=_cparams(("parallel",)), name="norm_proj",
    )(x, g.reshape(1, d), *ws)


def _qk_prep_kernel(q_ref, k_ref, v_ref, qn_ref, kn_ref, cos_ref, sin_ref, bd_ref,
                    qo_ref, ko_ref, kb_ref, vb_ref, *, hp, q_scale):
    tm = q_ref.shape[0]
    lane = lax.broadcasted_iota(jnp.int32, (tm, DA_W), 1)
    first_half = (lane % DA_DH) < (DA_DH // 2)
    bd = bd_ref[...]
    cos, sin = cos_ref[...], sin_ref[...]

    def norm_rope(x, g):
        sq = x * x
        ss = _dot_exact_rhs(sq, bd) if hp else _mm(sq.astype(BF16), bd)
        y = x * lax.rsqrt(ss * (1.0 / DA_DH) + EPS) * g
        partner = jnp.where(first_half, pltpu.roll(y, DA_W - DA_DH // 2, axis=1), pltpu.roll(y, DA_DH // 2, axis=1))
        return y * cos + partner * sin

    q = norm_rope(q_ref[...], qn_ref[...]) * q_scale
    k = norm_rope(k_ref[...], kn_ref[...])
    qo_ref[...] = q.astype(qo_ref.dtype)
    ko_ref[...] = k
    kb_ref[...] = k.astype(BF16)
    vb_ref[...] = v_ref[...].astype(BF16)


def _rope_tables(pos):
    half = DA_DH // 2
    inv = ROPE_THETA ** (-jnp.arange(half, dtype=F32) / half)
    ang = pos.astype(F32)[:, None] * inv[None, :]
    cos, sin = jnp.cos(ang), jnp.sin(ang)
    cos = jnp.tile(jnp.concatenate([cos, cos], axis=-1), (1, 2 * DA_H))
    sin = jnp.tile(jnp.concatenate([-sin, sin], axis=-1), (1, 2 * DA_H))
    return cos, sin


def _qk_prep(q_raw, k_raw, v_raw, qn, kn, cos, sin, *, hp, tm, q_dtype):
    n = q_raw.shape[0]
    t = cos.shape[0]
    if t % tm:
        cos, sin = jnp.tile(cos, (n // t, 1)), jnp.tile(sin, (n // t, 1))
        t = n
    assert n % tm == 0 and t % tm == 0
    nt = t // tm
    idx = jnp.arange(DA_W) // DA_DH
    bd = (idx[:, None] == idx[None, :]).astype(BF16)
    row = pl.BlockSpec((tm, DA_W), lambda i: (i, 0))
    vec = pl.BlockSpec((1, DA_W), lambda i: (0, 0))
    tab = pl.BlockSpec((tm, DA_W), lambda i: (i % nt, 0))
    return pl.pallas_call(
        functools.partial(_qk_prep_kernel, hp=hp, q_scale=DA_DH ** -0.5),
        grid=(n // tm,),
        in_specs=[row, row, row, vec, vec, tab, tab, pl.BlockSpec((DA_W, DA_W), lambda i: (0, 0))],
        out_specs=[row, row, row, row],
        out_shape=[jax.ShapeDtypeStruct((n, DA_W), q_dtype), jax.ShapeDtypeStruct((n, DA_W), F32),
                   jax.ShapeDtypeStruct((n, DA_W), BF16), jax.ShapeDtypeStruct((n, DA_W), BF16)],
        compiler_params=_cparams(("parallel",)), name="qk_prep",
    )(q_raw, k_raw, v_raw, jnp.tile(qn, 2 * DA_H).reshape(1, DA_W), jnp.tile(kn, 2 * DA_H).reshape(1, DA_W), cos, sin, bd)


def _lambda(lq1, lk1, lq2, lk2, lam_init):
    return (jnp.exp(jnp.sum(lq1[...] * lk1[...], axis=-1, keepdims=True))
            - jnp.exp(jnp.sum(lq2[...] * lk2[...], axis=-1, keepdims=True)) + lam_init)


def _flash_kernel(q_ref, k_ref, v_ref, lq1, lk1, lq2, lk2, sg_ref, o_ref, qs_scr, m_scr, l_scr, acc_scr,
                  *, blk, lam_init):
    i, j = pl.program_id(1), pl.program_id(2)
    hw = 2 * DA_DH

    @pl.when(j == 0)
    def _():
        q = q_ref[...]
        lane = lax.broadcasted_iota(jnp.int32, (blk, hw), 1)
        zero = jnp.zeros((blk, hw), q.dtype)
        for h in range(DA_H):
            qh = q[:, hw * h:hw * (h + 1)]
            qs_scr[h, 0:blk, :] = jnp.where(lane < DA_DH, qh, zero)
            qs_scr[h, blk:2 * blk, :] = jnp.where(lane >= DA_DH, qh, zero)
        m_scr[...] = jnp.full(m_scr.shape, -jnp.inf, F32)
        l_scr[...] = jnp.zeros(l_scr.shape, F32)
        acc_scr[...] = jnp.zeros(acc_scr.shape, F32)

    def step(diag):
        if diag:
            r = lax.broadcasted_iota(jnp.int32, (2 * blk, blk), 0)
            c = lax.broadcasted_iota(jnp.int32, (2 * blk, blk), 1)
            keep = c <= jnp.where(r >= blk, r - blk, r)
        for h in range(DA_H):
            s = _mm(qs_scr[h], k_ref[:, hw * h:hw * (h + 1)], _NT)
            if diag:
                s = jnp.where(keep, s, -jnp.inf)
            m_prev = m_scr[h][:, :1]
            m_new = jnp.maximum(m_prev, jnp.max(s, axis=-1, keepdims=True))
            alpha = jnp.exp(m_prev - m_new)
            p = jnp.exp(s - m_new)
            l_new = alpha * l_scr[h][:, :1] + jnp.sum(p, axis=-1, keepdims=True)
            acc_scr[h] = alpha * acc_scr[h] + _mm(p.astype(BF16), v_ref[:, hw * h:hw * (h + 1)])
            m_scr[h] = jnp.broadcast_to(m_new, (2 * blk, LANES))
            l_scr[h] = jnp.broadcast_to(l_new, (2 * blk, LANES))

    @pl.when(j < i)
    def _():
        step(False)

    @pl.when(j == i)
    def _():
        step(True)
        lam = _lambda(lq1, lk1, lq2, lk2, lam_init)
        for h in range(DA_H):
            acc = acc_scr[h]
            l = l_scr[h][:, :1]
            o = acc[0:blk] / l[0:blk] - lam * (acc[blk:] / l[blk:])
            o_ref[:, hw * h:hw * (h + 1)] = _rms(o, sg_ref[...]) * (1.0 - lam_init)


def _flash(q, k, v, lams, subln, *, batch, blk, lam_init):
    n = q.shape[0]
    nq = n // batch // blk
    hw = 2 * DA_DH
    qspec = pl.BlockSpec((blk, DA_W), lambda b, i, j: (b * nq + i, 0))
    kspec = pl.BlockSpec((blk, DA_W), lambda b, i, j: (b * nq + jnp.minimum(j, i), 0))
    lspec = pl.BlockSpec((1, DA_DH), lambda b, i, j: (0, 0))
    return pl.pallas_call(
        functools.partial(_flash_kernel, blk=blk, lam_init=lam_init),
        grid=(batch, nq, nq),
        in_specs=[qspec, kspec, kspec, lspec, lspec, lspec, lspec, pl.BlockSpec((1, hw), lambda b, i, j: (0, 0))],
        out_specs=qspec,
        out_shape=jax.ShapeDtypeStruct((n, DA_W), F32),
        scratch_shapes=[pltpu.VMEM((DA_H, 2 * blk, hw), BF16), pltpu.VMEM((DA_H, 2 * blk, LANES), F32),
                        pltpu.VMEM((DA_H, 2 * blk, LANES), F32), pltpu.VMEM((DA_H, 2 * blk, hw), F32)],
        compiler_params=_cparams(("parallel", "parallel", "arbitrary")), name="flash_diff_attn",
    )(q, k, v, *[x.reshape(1, DA_DH) for x in lams], subln.reshape(1, hw))


def _decode_kernel(pt_ref, qbd_ref, q8_ref, kn_ref, vn_ref, lq1, lk1, lq2, lk2, sg_ref, *refs,
                   group, lam_init):
    k_refs, v_refs = refs[:group], refs[group:2 * group]
    o_ref, m_scr, l_scr, acc_scr = refs[2 * group:]
    s_idx = pl.program_id(1)
    n_maps = 2 * DA_H

    @pl.when(s_idx == 0)
    def _():
        s0 = jnp.sum(q8_ref[...] * kn_ref[...], axis=-1, keepdims=True)
        m_scr[...] = jnp.broadcast_to(s0, (n_maps, LANES))
        l_scr[...] = jnp.ones((n_maps, LANES), F32)
        for h in range(DA_H):
            acc_scr[h] = jnp.broadcast_to(vn_ref[h:h + 1, :], (n_maps, 2 * DA_DH))

    q_hi, q_lo = _split(qbd_ref[...])
    a16 = jnp.concatenate([q_hi, q_lo], axis=0)

    s_list = []
    for g in range(group):
        kt = k_refs[g][...].reshape(n_maps * DA_DH, PAGE)
        k_hi, k_lo = _split(kt)
        r = _mm(a16, k_hi) + _mm(a16, k_lo)
        s_list.append(r[0:n_maps] + r[n_maps:])
    m_prev = m_scr[...][:, :1]
    m_cur = s_list[0].max(axis=-1, keepdims=True)
    for s in s_list[1:]:
        m_cur = jnp.maximum(m_cur, s.max(axis=-1, keepdims=True))
    m_new = jnp.maximum(m_prev, m_cur)
    alpha = jnp.exp(m_prev - m_new)
    l_new = alpha * l_scr[...][:, :1]
    acc = [alpha * acc_scr[h] for h in range(DA_H)]
    for g in range(group):
        p = jnp.exp(s_list[g] - m_new)
        l_new = l_new + jnp.sum(p, axis=-1, keepdims=True)
        p_hi, p_lo = _split(p)
        p16 = jnp.concatenate([p_hi, p_lo], axis=0)
        for h in range(DA_H):
            v_hi, v_lo = _split(v_refs[g][pl.ds(h, PAGE, stride=DA_H), :])
            r = _mm(p16, v_hi) + _mm(p16, v_lo)
            acc[h] = acc[h] + (r[0:n_maps] + r[n_maps:])
    m_scr[...] = jnp.broadcast_to(m_new, (n_maps, LANES))
    l_scr[...] = jnp.broadcast_to(l_new, (n_maps, LANES))
    for h in range(DA_H):
        acc_scr[h] = acc[h]

    @pl.when(s_idx == pl.num_programs(1) - 1)
    def _():
        lam = _lambda(lq1, lk1, lq2, lk2, lam_init)
        for h in range(DA_H):
            o1 = acc[h][2 * h:2 * h + 1] / l_new[2 * h:2 * h + 1]
            o2 = acc[h][2 * h + 1:2 * h + 2] / l_new[2 * h + 1:2 * h + 2]
            o_ref[h:h + 1, :] = _rms(o1 - lam * o2, sg_ref[...]) * (1.0 - lam_init)


def _decode_attn(q, k_new, v_new, kpool, vpool, page_table, layer, lams, subln, *, lam_init, group):
    s, n_pages = page_table.shape
    assert n_pages % group == 0
    hw = 2 * DA_DH
    n_maps = 2 * DA_H
    q8 = q.reshape(s, n_maps, DA_DH)
    qbd = (q8[:, :, None, :] * jnp.eye(n_maps, dtype=F32)[None, :, :, None]).reshape(s, n_maps, DA_W)
    kn8 = k_new.reshape(s, n_maps, DA_DH)
    vn4 = v_new.reshape(s, DA_H, hw)

    small = lambda shape: pl.BlockSpec((None,) + shape, lambda b, t, pt: (b, 0, 0))
    lspec = pl.BlockSpec((1, DA_DH), lambda b, t, pt: (0, 0))
    in_specs = [small((n_maps, DA_W)), small((n_maps, DA_DH)), small((n_maps, DA_DH)), small((DA_H, hw)),
                lspec, lspec, lspec, lspec, pl.BlockSpec((1, hw), lambda b, t, pt: (0, 0))]
    in_specs += [pl.BlockSpec((None, None, n_maps, DA_DH, PAGE),
                              lambda b, t, pt, g=g: (layer, pt[b, t * group + g], 0, 0, 0)) for g in range(group)]
    in_specs += [pl.BlockSpec((None, None, PAGE * DA_H, hw),
                              lambda b, t, pt, g=g: (layer, pt[b, t * group + g], 0, 0)) for g in range(group)]
    out = pl.pallas_call(
        functools.partial(_decode_kernel, group=group, lam_init=lam_init),
        grid_spec=pltpu.PrefetchScalarGridSpec(
            num_scalar_prefetch=1, grid=(s, n_pages // group), in_specs=in_specs,
            out_specs=small((DA_H, hw)),
            scratch_shapes=[pltpu.VMEM((n_maps, LANES), F32), pltpu.VMEM((n_maps, LANES), F32),
                            pltpu.VMEM((DA_H, n_maps, hw), F32)]),
        out_shape=jax.ShapeDtypeStruct((s, DA_H, hw), F32),
        compiler_params=_cparams(("parallel", "arbitrary")), name="decode_diff_attn",
    )(page_table, qbd, q8, kn8, vn4, *[x.reshape(1, DA_DH) for x in lams], subln.reshape(1, hw),
      *([kpool] * group), *([vpool] * group))
    return out.reshape(s, DA_W)


def _tri_inverse(low, eye):
    a = -low
    t = eye + a
    p = a
    steps = int(math.log2(low.shape[0])) - 1
    for _ in range(steps):
        p = _dot(p, p, True)
        t = t + _dot(t, p, True)
    return t


def _deltanet_kernel(qkv_ref, z_ref, ba_ref, conv0_ref, s0_ref, cw_ref, alog_ref, dtb_ref, ng_ref,
                     o_ref, s_out_ref, xbuf, s_scr, *, tt, t_real, hp):
    t = pl.program_id(1)
    c = DN_CHUNK

    @pl.when(t == 0)
    def _():
        xbuf[0:8, :] = conv0_ref[...]
        s_scr[...] = s0_ref[...]

    @pl.when(t > 0)
    def _():
        xbuf[0:8, :] = xbuf[tt:tt + 8, :]

    xbuf[8:8 + tt, :] = qkv_ref[...]
    cw = cw_ref[...]
    acc = xbuf[pl.ds(8 - (CONV_W - 1), tt), :] * cw[0:1]
    for jj in range(1, CONV_W):
        acc = acc + xbuf[pl.ds(8 - (CONV_W - 1) + jj, tt), :] * cw[jj:jj + 1]
    act = _silu(acc)

    ba = ba_ref[...]
    beta_all = jax.nn.sigmoid(ba)
    sp_in = ba + dtb_ref[...]
    softplus = jnp.maximum(sp_in, 0.0) + jnp.log(1.0 + jnp.exp(-jnp.abs(sp_in)))
    g_all = -jnp.exp(alog_ref[...]) * softplus
    if t_real is not None:
        ridx = t * tt + lax.broadcasted_iota(jnp.int32, (tt, 1), 0)
        live = ridx < t_real
        act = jnp.where(live, act, 0.0)
        beta_all = jnp.where(live, beta_all, 0.0)
        g_all = jnp.where(live, g_all, 0.0)

    ri = lax.broadcasted_iota(jnp.int32, (c, c), 0)
    ci = lax.broadcasted_iota(jnp.int32, (c, c), 1)
    causal, strict = ri >= ci, ri > ci
    eye = jnp.where(ri == ci, 1.0, 0.0).astype(F32)
    ltri = jnp.where(causal, 1.0, 0.0).astype(BF16)
    ng = ng_ref[...]

    for cc in range(tt // c):
        rs = slice(cc * c, (cc + 1) * c)
        g_c = g_all[rs]
        gc_all = _dot_exact_lhs(ltri, g_c)
        prep = []
        for h in range(DN_H):
            q = act[rs, DN_DK * h:DN_DK * (h + 1)]
            k = act[rs, DN_QK + DN_DK * h:DN_QK + DN_DK * (h + 1)]
            v = act[rs, 2 * DN_QK + DN_DV * h:2 * DN_QK + DN_DV * (h + 1)]
            q = q * lax.rsqrt(jnp.sum(q * q, axis=-1, keepdims=True) + EPS) * (DN_DK ** -0.5)
            k = k * lax.rsqrt(jnp.sum(k * k, axis=-1, keepdims=True) + EPS)
            beta = beta_all[rs, h:h + 1]
            g_col = g_c[:, DN_H + h:DN_H + h + 1]
            gc = gc_all[:, DN_H + h:DN_H + h + 1]
            gdiff = _dot_exact_lhs(ltri, jnp.where(strict, g_col, 0.0))
            decay = jnp.where(causal, jnp.exp(gdiff), 0.0)
            kb, vb = k * beta, v * beta
            low = jnp.where(strict, _dot(kb, k, hp, _NT) * decay, 0.0)
            tinv = _tri_inverse(low, eye)
            eg = jnp.exp(gc)
            sol = _dot(tinv, jnp.concatenate([vb, kb * eg], axis=-1), True)
            u, w = sol[:, :DN_DV], sol[:, DN_DV:]
            qk = _dot(q, k, hp, _NT) * decay
            gc_last = gc[c - 1:c, :]
            prep.append((u, w, qk, q * eg, k * jnp.exp(gc_last - gc), jnp.exp(gc_last)))
        for h in range(DN_H):
            u, w, qk, qd, kd, gl = prep[h]
            s = s_scr[h]
            v_new = u - _dot(w, s, hp)
            o = _dot(qd, s, hp) + _dot(qk, v_new, hp)
            s_scr[h] = s * gl + _dot(jnp.transpose(kd), v_new, hp)
            zz = z_ref[rs, DN_DV * h:DN_DV * (h + 1)]
            o_ref[rs, DN_DV * h:DN_DV * (h + 1)] = _rms(o, ng) * _silu(zz)

    @pl.when(t == pl.num_programs(1) - 1)
    def _():
        s_out_ref[...] = s_scr[...]


def _deltanet(qkv, z, ba, conv0, s0, conv_w, a_log, dt_bias, norm_g, *, batch, t_pad, t_real, tt, hp):
    n = qkv.shape[0]
    assert n == batch * t_pad and t_pad % tt == 0 and tt % DN_CHUNK == 0
    nt = t_pad // tt
    conv0p = jnp.pad(conv0, ((0, 0), (8 - (CONV_W - 1), 0), (0, 0)))
    cwp = jnp.pad(conv_w, ((0, 8 - CONV_W), (0, 0)))
    lane_vec = lambda a: jnp.pad(a, (DN_H, LANES - 2 * DN_H)).reshape(1, LANES)
    row = lambda w: pl.BlockSpec((tt, w), lambda b, t: (b * nt + t, 0))
    const = lambda shape: pl.BlockSpec(shape, lambda b, t: (0,) * len(shape))
    a_out, s_out = pl.pallas_call(
        functools.partial(_deltanet_kernel, tt=tt, t_real=None if t_real == t_pad else t_real, hp=hp),
        grid=(batch, nt),
        in_specs=[row(DN_CH), row(DN_QK), row(LANES),
                  pl.BlockSpec((None, 8, DN_CH), lambda b, t: (b, 0, 0)),
                  pl.BlockSpec((None, DN_H, DN_DK, DN_DV), lambda b, t: (b, 0, 0, 0)),
                  const((8, DN_CH)), const((1, LANES)), const((1, LANES)), const((1, DN_DV))],
        out_specs=[row(DN_QK), pl.BlockSpec((None, DN_H, DN_DK, DN_DV), lambda b, t: (b, 0, 0, 0))],
        out_shape=[jax.ShapeDtypeStruct((n, DN_QK), F32), jax.ShapeDtypeStruct(s0.shape, F32)],
        scratch_shapes=[pltpu.VMEM((tt + 8, DN_CH), F32), pltpu.VMEM((DN_H, DN_DK, DN_DV), F32)],
        compiler_params=_cparams(("parallel", "arbitrary")), name="deltanet",
    )(qkv, z, ba, conv0p, s0, cwp, lane_vec(a_log), lane_vec(dt_bias), norm_g.reshape(1, DN_DV))
    return a_out, s_out


def _mix_kernel(x_ref, a_ref, b_ref, ga_ref, gb_ref, wa_ref, wb_ref, wo_ref, fn_ref, *refs, hp, n_experts):
    if n_experts:
        wr_ref, x1_ref, h_ref, comb_ref = refs
    else:
        x1_ref, h_ref = refs
    a = _dot(a_ref[...], wa_ref[...], hp)
    b = _dot(b_ref[...], wb_ref[...], hp)
    mixed = jax.nn.sigmoid(ga_ref[...]) * a + jax.nn.sigmoid(gb_ref[...]) * b
    x1 = x_ref[...] + _dot(mixed, wo_ref[...], hp)
    x1_ref[...] = x1
    h = _rms(x1, fn_ref[...])
    h_ref[...] = h.astype(h_ref.dtype)
    if n_experts:
        logits = _dot(h, wr_ref[...], True)
        lane = lax.broadcasted_iota(jnp.int32, logits.shape, 1).astype(F32)
        lg = jnp.where(lane < n_experts, logits, -jnp.inf)
        m1 = jnp.max(lg, axis=-1, keepdims=True)
        i1 = jnp.min(jnp.where(lg == m1, lane, float(LANES)), axis=-1, keepdims=True)
        lg2 = jnp.where(lane == i1, -jnp.inf, lg)
        m2 = jnp.max(lg2, axis=-1, keepdims=True)
        i2 = jnp.min(jnp.where(lg2 == m2, lane, float(LANES)), axis=-1, keepdims=True)
        e2 = jnp.exp(m2 - m1)
        den = 1.0 + e2
        comb_ref[...] = jnp.where(lane == i1, 1.0 / den, 0.0) + jnp.where(lane == i2, e2 / den, 0.0)


def _mix(x, a_out, b_out, g_raw, wa, wb, wo, fn, w_router, *, hp, tm, h_dtype):
    n, d = x.shape
    assert n % tm == 0
    n_experts = 0 if w_router is None else w_router.shape[1]
    row = lambda w, c=0: pl.BlockSpec((tm, w), lambda i, c=c: (i, c))
    const = lambda shape: pl.BlockSpec(shape, lambda i: (0, 0))
    in_specs = [row(d), row(a_out.shape[1]), row(b_out.shape[1]), row(d, 0), row(d, 1),
                const(wa.shape), const(wb.shape), const(wo.shape), const((1, d))]
    args = [x, a_out, b_out, g_raw, g_raw, wa, wb, wo, fn.reshape(1, d)]
    out_specs = [row(d), row(d)]
    out_shape = [jax.ShapeDtypeStruct((n, d), F32), jax.ShapeDtypeStruct((n, d), h_dtype)]
    if n_experts:
        in_specs.append(const((d, LANES)))
        args.append(jnp.pad(w_router, ((0, 0), (0, LANES - n_experts))))
        out_specs.append(row(LANES))
        out_shape.append(jax.ShapeDtypeStruct((n, LANES), F32))
    return pl.pallas_call(
        functools.partial(_mix_kernel, hp=hp, n_experts=n_experts),
        grid=(n // tm,), in_specs=in_specs, out_specs=out_specs, out_shape=out_shape,
        compiler_params=_cparams(("parallel",)), name="mix_out",
    )(*args)


def _ffn_kernel(x1_ref, h_ref, *refs, hp, use_comb):
    if use_comb:
        comb_ref, wg_ref, wu_ref, wd_ref, o_ref = refs
    else:
        wg_ref, wu_ref, wd_ref, o_ref = refs
    e, f = pl.program_id(1), pl.program_id(2)

    @pl.when((e == 0) & (f == 0))
    def _():
        o_ref[...] = x1_ref[...]

    h = h_ref[...]
    act = _silu(_dot(h, wg_ref[...], hp)) * _dot(h, wu_ref[...], hp)
    y = _dot(act, wd_ref[...], hp)
    if use_comb:
        comb = comb_ref[...]
        lane = lax.broadcasted_iota(jnp.int32, comb.shape, 1)
        y = y * jnp.sum(jnp.where(lane == e, comb, 0.0), axis=-1, keepdims=True)
    o_ref[...] += y


def _ffn(x1, h, comb, w_gu, w_down, *, hp, tm, tf):
    n, d = x1.shape
    n_e, f_dim = w_down.shape[0], w_down.shape[1]
    assert n % tm == 0 and f_dim % tf == 0
    nf = f_dim // tf
    row = lambda w: pl.BlockSpec((tm, w), lambda i, e, f: (i, 0))
    in_specs = [row(d), row(d)]
    args = [x1, h]
    if comb is not None:
        in_specs.append(row(LANES))
        args.append(comb)
    in_specs += [pl.BlockSpec((None, d, tf), lambda i, e, f: (e, 0, f)),
                 pl.BlockSpec((None, d, tf), lambda i, e, f: (e, 0, f + nf)),
                 pl.BlockSpec((None, tf, d), lambda i, e, f: (e, f, 0))]
    args += [w_gu, w_gu, w_down]
    return pl.pallas_call(
        functools.partial(_ffn_kernel, hp=hp, use_comb=comb is not None),
        grid=(n // tm, n_e, nf), in_specs=in_specs, out_specs=row(d),
        out_shape=jax.ShapeDtypeStruct((n, d), F32),
        compiler_params=_cparams(("parallel", "arbitrary", "arbitrary")), name="ffn",
    )(*args)


def _split_w_in(w):
    o = [0]
    for s in (DN_CH, DN_QK, DN_H, DN_H, DA_W, DA_W, DA_W):
        o.append(o[-1] + s)
    ba = jnp.pad(w[:, o[2]:o[4]], ((0, 0), (0, LANES - 2 * DN_H)))
    return [w[:, o[0]:o[1]], w[:, o[1]:o[2]], ba, w[:, o[4]:o[5]], w[:, o[5]:o[6]], w[:, o[6]:o[7]], w[:, o[7]:]]


def _trunk(x, pos0, conv0, ssm0, paged, p, *, hp):
    b, t, d = x.shape
    n = b * t
    depth = p['w_in'].shape[0]
    wdt = F32 if hp else BF16
    cast = lambda w: w.astype(wdt)
    x2 = x.reshape(n, d)
    tm = min(256, n)
    cos, sin = _rope_tables(pos0 + jnp.arange(t, dtype=jnp.int32))
    t_pad = -(-t // DN_CHUNK) * DN_CHUNK
    tt = min(128, t_pad)
    k_rows, v_rows, ssm_out, conv_out = [], [], [], []
    for l in range(depth):
        lam_init = 0.8 - 0.6 * math.exp(-0.3 * l)
        ws = [cast(w) for w in _split_w_in(p['w_in'][l])]
        if hp:
            outs = []
            for grp in (ws[0:3], ws[3:6], ws[6:7]):
                outs += _norm_proj(x2, p['attn_norm'][l], grp, hp=hp, tm=tm)
        else:
            outs = _norm_proj(x2, p['attn_norm'][l], ws, hp=hp, tm=tm)
        qkv_pre, z, ba, q_raw, k_raw, v_raw, g_raw = outs

        q_s, k_rot, k_bf, v_bf = _qk_prep(q_raw, k_raw, v_raw, p['q_norm'][l], p['k_norm'][l], cos, sin,
                                          hp=hp, tm=tm, q_dtype=F32 if hp else BF16)
        lams = (p['lambda_q1'][l], p['lambda_k1'][l], p['lambda_q2'][l], p['lambda_k2'][l])
        if paged is None:
            b_out = _flash(q_s, k_bf, v_bf, lams, p['subln'][l], batch=b, blk=min(512, t), lam_init=lam_init)
        else:
            kpool, vpool, page_table = paged
            b_out = _decode_attn(q_s, k_rot, v_raw, kpool, vpool, page_table, l, lams, p['subln'][l],
                                 lam_init=lam_init, group=8)

        def pad_t(a):
            if t_pad == t:
                return a
            return jnp.pad(a.reshape(b, t, -1), ((0, 0), (0, t_pad - t), (0, 0))).reshape(b * t_pad, -1)

        a_out, s_new = _deltanet(pad_t(qkv_pre), pad_t(z), pad_t(ba), conv0[l], ssm0[l], p['conv_w'][l],
                                 p['a_log'][l], p['dt_bias'][l], p['dn_norm'][l],
                                 batch=b, t_pad=t_pad, t_real=t, tt=tt, hp=hp)
        if t_pad != t:
            a_out = a_out.reshape(b, t_pad, -1)[:, :t].reshape(n, -1)

        moe = l % 2 == 1
        mixed = _mix(x2, a_out, b_out, g_raw, cast(p['w_a_proj'][l]), cast(p['w_b_proj'][l]), cast(p['w_out'][l]),
                     p['ffn_norm'][l], p['w_router'][l // 2] if moe else None, hp=hp, tm=tm, h_dtype=wdt)
        if moe:
            x1, h2, comb = mixed
            x2 = _ffn(x1, h2, comb, cast(p['w_exp_gu'][l // 2]), cast(p['w_exp_down'][l // 2]),
                      hp=hp, tm=min(512, n), tf=p['w_exp_down'].shape[2])
        else:
            x1, h2 = mixed
            x2 = _ffn(x1, h2, None, cast(p['w_ffn_gu'][l // 2][None]), cast(p['w_ffn_down'][l // 2][None]),
                      hp=hp, tm=min(512, n), tf=p['w_ffn_down'].shape[1] // 2)

        k_rows.append(k_rot.reshape(b, t, 2 * DA_H, DA_DH))
        v_rows.append(v_raw.reshape(b, t, DA_H, 2 * DA_DH))
        ssm_out.append(s_new)
        tail = qkv_pre.reshape(b, t, DN_CH)[:, max(t - (CONV_W - 1), 0):]
        conv_out.append(jnp.concatenate([conv0[l], tail], axis=1)[:, -(CONV_W - 1):])
    return x2.reshape(b, t, d), jnp.stack(k_rows), jnp.stack(v_rows), jnp.stack(ssm_out), jnp.stack(conv_out)


def kernel(x_prompt, x_sample, cache_k, cache_v, state_ssm, state_conv, page_table, attn_norm, w_in, conv_w,
           a_log, dt_bias, dn_norm, q_norm, k_norm, lambda_q1, lambda_k1, lambda_q2, lambda_k2, subln,
           w_a_proj, w_b_proj, w_out, ffn_norm, w_ffn_gu, w_ffn_down, w_router, w_exp_gu, w_exp_down):
    p = dict(attn_norm=attn_norm, w_in=w_in, conv_w=conv_w, a_log=a_log, dt_bias=dt_bias, dn_norm=dn_norm,
             q_norm=q_norm, k_norm=k_norm, lambda_q1=lambda_q1, lambda_k1=lambda_k1, lambda_q2=lambda_q2,
             lambda_k2=lambda_k2, subln=subln, w_a_proj=w_a_proj, w_b_proj=w_b_proj, w_out=w_out,
             ffn_norm=ffn_norm, w_ffn_gu=w_ffn_gu, w_ffn_down=w_ffn_down, w_router=w_router,
             w_exp_gu=w_exp_gu, w_exp_down=w_exp_down)
    depth = w_in.shape[0]
    bp = x_prompt.shape[0]
    assert x_sample.shape[1] == 1, "the decode path handles one new token per sequence"
    assert cache_k.shape[2] == PAGE
    conv0 = jnp.zeros((depth, bp, CONV_W - 1, DN_CH), F32)
    ssm0 = jnp.zeros((depth, bp, DN_H, DN_DK, DN_DV), F32)
    y_p, k_p, v_p, ssm_p, conv_p = _trunk(x_prompt, 0, conv0, ssm0, None, p, hp=False)

    past_len = page_table.shape[1] * PAGE
    n_pool = cache_k.shape[1]
    kpool = jnp.transpose(cache_k, (0, 1, 3, 4, 2))
    vpool = cache_v.reshape(depth, n_pool, PAGE * DA_H, 2 * DA_DH)
    y_s, k_s, v_s, ssm_s, conv_s = _trunk(x_sample, past_len, state_conv, state_ssm, (kpool, vpool, page_table), p, hp=True)
    return (y_p, y_s, k_p, v_p, ssm_p, conv_p, k_s, v_s, ssm_s, conv_s)
```

```python
import functools
import math

import jax
import jax.numpy as jnp
from jax import lax
from jax.experimental import pallas as pl
from jax.experimental.pallas import tpu as pltpu

F32 = jnp.float32
BF16 = jnp.bfloat16
EPS = 1e-6
LANES = 128
VMEM_LIMIT = 56 * 1024 * 1024

DN_H, DN_DK, DN_DV = 4, 128, 128
DN_QK = DN_H * DN_DK
DN_CH = 3 * DN_QK
CONV_W = 4
DN_CHUNK = 64
DA_H, DA_DH = 4, 64
DA_W = 2 * DA_H * DA_DH
ROPE_THETA = 10000.0
PAGE = 128

_NN = (((1,), (0,)), ((), ()))
_NT = (((1,), (1,)), ((), ()))
_BNN = (((2,), (1,)), ((0,), (0,)))
_BNT = (((2,), (2,)), ((0,), (0,)))
assert 2 * DA_DH == LANES and DN_DV == LANES


def _cparams(sem):
    return pltpu.CompilerParams(dimension_semantics=sem, vmem_limit_bytes=VMEM_LIMIT)


def _mm(a, b, dims=_NN):
    return lax.dot_general(a, b, dims, preferred_element_type=F32)


def _split(a):
    if a.dtype == BF16:
        return a, None
    hi = a.astype(BF16)
    lo = (a - hi.astype(F32)).astype(BF16)
    return hi, lo


def _dot(a, b, hp, dims=_NN):
    if not hp:
        return _mm(a.astype(BF16), b.astype(BF16), dims)
    ah, al = _split(a)
    bh, bl = _split(b)
    out = _mm(ah, bh, dims)
    if al is not None:
        out = out + _mm(al, bh, dims)
    if bl is not None:
        out = out + _mm(ah, bl, dims)
    return out


def _split3(a):
    hi = a.astype(BF16)
    r = a - hi.astype(F32)
    mid = r.astype(BF16)
    lo = (r - mid.astype(F32)).astype(BF16)
    return hi, mid, lo


def _dot_exact_lhs(a_bf16, b, dims=_NN):
    hi, mid, lo = _split3(b)
    return _mm(a_bf16, hi, dims) + (_mm(a_bf16, mid, dims) + _mm(a_bf16, lo, dims))


def _dot_exact_rhs(a, b_bf16):
    hi, mid, lo = _split3(a)
    return _mm(hi, b_bf16) + (_mm(mid, b_bf16) + _mm(lo, b_bf16))


def _rms(x, g):
    return x * lax.rsqrt(jnp.mean(x * x, axis=-1, keepdims=True) + EPS) * g


def _silu(x):
    return x * jax.nn.sigmoid(x)


def _norm_proj_kernel(x_ref, g_ref, *refs, n_w, hp):
    w_refs, o_refs = refs[:n_w], refs[n_w:]
    h = _rms(x_ref[...], g_ref[...])
    if not hp:
        h = h.astype(BF16)
    for w_ref, o_ref in zip(w_refs, o_refs):
        o_ref[...] = _dot(h, w_ref[...], hp).astype(o_ref.dtype)


def _norm_proj(x, g, ws, *, hp, tm):
    n, d = x.shape
    assert n % tm == 0
    in_specs = [pl.BlockSpec((tm, d), lambda i: (i, 0)), pl.BlockSpec((1, d), lambda i: (0, 0))]
    in_specs += [pl.BlockSpec(w.shape, lambda i: (0, 0)) for w in ws]
    out_specs = [pl.BlockSpec((tm, w.shape[1]), lambda i: (i, 0)) for w in ws]
    out_shape = [jax.ShapeDtypeStruct((n, w.shape[1]), F32) for w in ws]
    return pl.pallas_call(
        functools.partial(_norm_proj_kernel, n_w=len(ws), hp=hp),
        grid=(n // tm,), in_specs=in_specs, out_specs=out_specs, out_shape=out_shape,
        compiler_params=_cparams(("parallel",)), name="norm_proj",
    )(x, g.reshape(1, d), *ws)


def _qk_prep_kernel(q_ref, k_ref, v_ref, qn_ref, kn_ref, cos_ref, sin_ref, bd_ref,
                    qo_ref, ko_ref, kb_ref, vb_ref, *, hp, q_scale):
    tm = q_ref.shape[0]
    lane = lax.broadcasted_iota(jnp.int32, (tm, DA_W), 1)
    first_half = (lane % DA_DH) < (DA_DH // 2)
    bd = bd_ref[...]
    cos, sin = cos_ref[...], sin_ref[...]

    def norm_rope(x, g):
        sq = x * x
        ss = _dot_exact_rhs(sq, bd) if hp else _mm(sq.astype(BF16), bd)
        y = x * lax.rsqrt(ss * (1.0 / DA_DH) + EPS) * g
        partner = jnp.where(first_half, pltpu.roll(y, DA_W - DA_DH // 2, axis=1), pltpu.roll(y, DA_DH // 2, axis=1))
        return y * cos + partner * sin

    q = norm_rope(q_ref[...], qn_ref[...]) * q_scale
    k = norm_rope(k_ref[...], kn_ref[...])
    qo_ref[...] = q.astype(qo_ref.dtype)
    ko_ref[...] = k
    kb_ref[...] = k.astype(BF16)
    vb_ref[...] = v_ref[...].astype(BF16)


def _rope_tables(pos):
    half = DA_DH // 2
    inv = ROPE_THETA ** (-jnp.arange(half, dtype=F32) / half)
    ang = pos.astype(F32)[:, None] * inv[None, :]
    cos, sin = jnp.cos(ang), jnp.sin(ang)
    cos = jnp.tile(jnp.concatenate([cos, cos], axis=-1), (1, 2 * DA_H))
    sin = jnp.tile(jnp.concatenate([-sin, sin], axis=-1), (1, 2 * DA_H))
    return cos, sin


def _qk_prep(q_raw, k_raw, v_raw, qn, kn, cos, sin, *, hp, tm, q_dtype):
    n = q_raw.shape[0]
    t = cos.shape[0]
    if t % tm:
        cos, sin = jnp.tile(cos, (n // t, 1)), jnp.tile(sin, (n // t, 1))
        t = n
    assert n % tm == 0 and t % tm == 0
    nt = t // tm
    idx = jnp.arange(DA_W) // DA_DH
    bd = (idx[:, None] == idx[None, :]).astype(BF16)
    row = pl.BlockSpec((tm, DA_W), lambda i: (i, 0))
    vec = pl.BlockSpec((1, DA_W), lambda i: (0, 0))
    tab = pl.BlockSpec((tm, DA_W), lambda i: (i % nt, 0))
    return pl.pallas_call(
        functools.partial(_qk_prep_kernel, hp=hp, q_scale=DA_DH ** -0.5),
        grid=(n // tm,),
        in_specs=[row, row, row, vec, vec, tab, tab, pl.BlockSpec((DA_W, DA_W), lambda i: (0, 0))],
        out_specs=[row, row, row, row],
        out_shape=[jax.ShapeDtypeStruct((n, DA_W), q_dtype), jax.ShapeDtypeStruct((n, DA_W), F32),
                   jax.ShapeDtypeStruct((n, DA_W), BF16), jax.ShapeDtypeStruct((n, DA_W), BF16)],
        compiler_params=_cparams(("parallel",)), name="qk_prep",
    )(q_raw, k_raw, v_raw, jnp.tile(qn, 2 * DA_H).reshape(1, DA_W), jnp.tile(kn, 2 * DA_H).reshape(1, DA_W), cos, sin, bd)


def _lambda(lq1, lk1, lq2, lk2, lam_init):
    return (jnp.exp(jnp.sum(lq1[...] * lk1[...], axis=-1, keepdims=True))
            - jnp.exp(jnp.sum(lq2[...] * lk2[...], axis=-1, keepdims=True)) + lam_init)


def _flash_kernel(q_ref, k_ref, v_ref, lq1, lk1, lq2, lk2, sg_ref, o_ref, qs_scr, m_scr, l_scr, acc_scr,
                  s_scr, p_scr, al_scr, *, blk, lam_init):
    i, j = pl.program_id(1), pl.program_id(2)
    hw = 2 * DA_DH
    rb = min(64, blk)

    @pl.when(j == 0)
    def _():
        q = q_ref[...]
        lane = lax.broadcasted_iota(jnp.int32, (blk, hw), 1)
        zero = jnp.zeros((blk, hw), q.dtype)
        for h in range(DA_H):
            qh = q[:, hw * h:hw * (h + 1)]
            qs_scr[h, 0:blk, :] = jnp.where(lane < DA_DH, qh, zero)
            qs_scr[h, blk:2 * blk, :] = jnp.where(lane >= DA_DH, qh, zero)
        m_scr[...] = jnp.full(m_scr.shape, -jnp.inf, F32)
        l_scr[...] = jnp.zeros(l_scr.shape, F32)
        acc_scr[...] = jnp.zeros(acc_scr.shape, F32)

    def step(diag):
        if diag:
            r = lax.broadcasted_iota(jnp.int32, (2 * blk, blk), 0)
            c = lax.broadcasted_iota(jnp.int32, (2 * blk, blk), 1)
            keep = c <= jnp.where(r >= blk, r - blk, r)
        for h in range(DA_H):
            s = _mm(qs_scr[h], k_ref[:, hw * h:hw * (h + 1)], _NT)
            if diag:
                s = jnp.where(keep, s, -jnp.inf)
            s_scr[h] = s
            m_prev = m_scr[h]
            m_new = jnp.maximum(m_prev, jnp.max(s, axis=-1, keepdims=True))
            m_scr[h] = m_new
            al_scr[h] = jnp.exp(m_prev - m_new)
            for t in range(2 * blk // rb):
                rows = slice(t * rb, (t + 1) * rb)
                m_rows = m_scr[h, rows, :]
                p = jnp.exp(s_scr[h, rows, :] - jnp.concatenate([m_rows] * (blk // LANES), axis=1))
                l_scr[h, rows, :] = al_scr[h, rows, :] * l_scr[h, rows, :] + jnp.sum(p, axis=-1, keepdims=True)
                p_scr[h, rows, :] = p.astype(BF16)
            acc_scr[h] = al_scr[h] * acc_scr[h] + _mm(p_scr[h], v_ref[:, hw * h:hw * (h + 1)])

    @pl.when(j < i)
    def _():
        step(False)

    @pl.when(j == i)
    def _():
        step(True)
        lam = _lambda(lq1, lk1, lq2, lk2, lam_init)
        for h in range(DA_H):
            acc = acc_scr[h]
            l = l_scr[h]
            o = acc[0:blk] / l[0:blk] - lam * (acc[blk:] / l[blk:])
            o_ref[:, hw * h:hw * (h + 1)] = _rms(o, sg_ref[...]) * (1.0 - lam_init)


def _flash(q, k, v, lams, subln, *, batch, blk, lam_init):
    n = q.shape[0]
    nq = n // batch // blk
    hw = 2 * DA_DH
    qspec = pl.BlockSpec((blk, DA_W), lambda b, i, j: (b * nq + i, 0))
    kspec = pl.BlockSpec((blk, DA_W), lambda b, i, j: (b * nq + jnp.minimum(j, i), 0))
    lspec = pl.BlockSpec((1, DA_DH), lambda b, i, j: (0, 0))
    return pl.pallas_call(
        functools.partial(_flash_kernel, blk=blk, lam_init=lam_init),
        grid=(batch, nq, nq),
        in_specs=[qspec, kspec, kspec, lspec, lspec, lspec, lspec, pl.BlockSpec((1, hw), lambda b, i, j: (0, 0))],
        out_specs=qspec,
        out_shape=jax.ShapeDtypeStruct((n, DA_W), F32),
        scratch_shapes=[pltpu.VMEM((DA_H, 2 * blk, hw), BF16), pltpu.VMEM((DA_H, 2 * blk, LANES), F32),
                        pltpu.VMEM((DA_H, 2 * blk, LANES), F32), pltpu.VMEM((DA_H, 2 * blk, hw), F32),
                        pltpu.VMEM((DA_H, 2 * blk, blk), F32), pltpu.VMEM((DA_H, 2 * blk, blk), BF16),
                        pltpu.VMEM((DA_H, 2 * blk, LANES), F32)],
        compiler_params=_cparams(("parallel", "parallel", "arbitrary")), name="flash_diff_attn",
    )(q, k, v, *[x.reshape(1, DA_DH) for x in lams], subln.reshape(1, hw))


def _decode_kernel(pt_ref, qbd_ref, q8_ref, kn_ref, vn_ref, lq1, lk1, lq2, lk2, sg_ref, *refs,
                   group, lam_init):
    k_refs, v_refs = refs[:group], refs[group:2 * group]
    o_ref, m_scr, l_scr, acc_scr = refs[2 * group:]
    s_idx = pl.program_id(1)
    n_maps = 2 * DA_H

    @pl.when(s_idx == 0)
    def _():
        s0 = jnp.sum(q8_ref[...] * kn_ref[...], axis=-1, keepdims=True)
        m_scr[...] = jnp.broadcast_to(s0, (n_maps, LANES))
        l_scr[...] = jnp.ones((n_maps, LANES), F32)
        for h in range(DA_H):
            acc_scr[h] = jnp.broadcast_to(vn_ref[h:h + 1, :], (n_maps, 2 * DA_DH))

    q_hi, q_lo = _split(qbd_ref[...])
    a16 = jnp.concatenate([q_hi, q_lo], axis=0)

    s_list = []
    for g in range(group):
        kt = k_refs[g][...].reshape(n_maps * DA_DH, PAGE)
        k_hi, k_lo = _split(kt)
        r = _mm(a16, k_hi) + _mm(a16, k_lo)
        s_list.append(r[0:n_maps] + r[n_maps:])
    m_prev = m_scr[...][:, :1]
    m_cur = s_list[0].max(axis=-1, keepdims=True)
    for s in s_list[1:]:
        m_cur = jnp.maximum(m_cur, s.max(axis=-1, keepdims=True))
    m_new = jnp.maximum(m_prev, m_cur)
    alpha = jnp.exp(m_prev - m_new)
    l_new = alpha * l_scr[...][:, :1]
    acc = [alpha * acc_scr[h] for h in range(DA_H)]
    for g in range(group):
        p = jnp.exp(s_list[g] - m_new)
        l_new = l_new + jnp.sum(p, axis=-1, keepdims=True)
        p_hi, p_lo = _split(p)
        p16 = jnp.concatenate([p_hi, p_lo], axis=0)
        for h in range(DA_H):
            v_hi, v_lo = _split(v_refs[g][pl.ds(h, PAGE, stride=DA_H), :])
            r = _mm(p16, v_hi) + _mm(p16, v_lo)
            acc[h] = acc[h] + (r[0:n_maps] + r[n_maps:])
    m_scr[...] = jnp.broadcast_to(m_new, (n_maps, LANES))
    l_scr[...] = jnp.broadcast_to(l_new, (n_maps, LANES))
    for h in range(DA_H):
        acc_scr[h] = acc[h]

    @pl.when(s_idx == pl.num_programs(1) - 1)
    def _():
        lam = _lambda(lq1, lk1, lq2, lk2, lam_init)
        for h in range(DA_H):
            o1 = acc[h][2 * h:2 * h + 1] / l_new[2 * h:2 * h + 1]
            o2 = acc[h][2 * h + 1:2 * h + 2] / l_new[2 * h + 1:2 * h + 2]
            o_ref[h:h + 1, :] = _rms(o1 - lam * o2, sg_ref[...]) * (1.0 - lam_init)


def _decode_attn(q, k_new, v_new, kpool, vpool, page_table, layer, lams, subln, *, lam_init, group):
    s, n_pages = page_table.shape
    assert n_pages % group == 0
    hw = 2 * DA_DH
    n_maps = 2 * DA_H
    q8 = q.reshape(s, n_maps, DA_DH)
    qbd = (q8[:, :, None, :] * jnp.eye(n_maps, dtype=F32)[None, :, :, None]).reshape(s, n_maps, DA_W)
    kn8 = k_new.reshape(s, n_maps, DA_DH)
    vn4 = v_new.reshape(s, DA_H, hw)

    small = lambda shape: pl.BlockSpec((None,) + shape, lambda b, t, pt: (b, 0, 0))
    lspec = pl.BlockSpec((1, DA_DH), lambda b, t, pt: (0, 0))
    in_specs = [small((n_maps, DA_W)), small((n_maps, DA_DH)), small((n_maps, DA_DH)), small((DA_H, hw)),
                lspec, lspec, lspec, lspec, pl.BlockSpec((1, hw), lambda b, t, pt: (0, 0))]
    in_specs += [pl.BlockSpec((None, None, n_maps, DA_DH, PAGE),
                              lambda b, t, pt, g=g: (layer, pt[b, t * group + g], 0, 0, 0)) for g in range(group)]
    in_specs += [pl.BlockSpec((None, None, PAGE * DA_H, hw),
                              lambda b, t, pt, g=g: (layer, pt[b, t * group + g], 0, 0)) for g in range(group)]
    out = pl.pallas_call(
        functools.partial(_decode_kernel, group=group, lam_init=lam_init),
        grid_spec=pltpu.PrefetchScalarGridSpec(
            num_scalar_prefetch=1, grid=(s, n_pages // group), in_specs=in_specs,
            out_specs=small((DA_H, hw)),
            scratch_shapes=[pltpu.VMEM((n_maps, LANES), F32), pltpu.VMEM((n_maps, LANES), F32),
                            pltpu.VMEM((DA_H, n_maps, hw), F32)]),
        out_shape=jax.ShapeDtypeStruct((s, DA_H, hw), F32),
        compiler_params=_cparams(("parallel", "arbitrary")), name="decode_diff_attn",
    )(page_table, qbd, q8, kn8, vn4, *[x.reshape(1, DA_DH) for x in lams], subln.reshape(1, hw),
      *([kpool] * group), *([vpool] * group))
    return out.reshape(s, DA_W)


def _tri_inverse(low, eye):
    a = -low
    t = eye + a
    p = a
    steps = int(math.log2(low.shape[-1])) - 1
    for _ in range(steps):
        p = _dot(p, p, True, _BNN)
        t = t + _dot(t, p, True, _BNN)
    return t


def _deltanet_kernel(qkv_ref, z_ref, ba_ref, conv0_ref, s0_ref, cw_ref, alog_ref, dtb_ref, ng_ref,
                     o_ref, s_out_ref, xbuf, s_scr, *, tt, t_real, hp):
    t = pl.program_id(1)
    c = DN_CHUNK

    @pl.when(t == 0)
    def _():
        xbuf[0:8, :] = conv0_ref[...]
        s_scr[...] = s0_ref[...]

    @pl.when(t > 0)
    def _():
        xbuf[0:8, :] = xbuf[tt:tt + 8, :]

    xbuf[8:8 + tt, :] = qkv_ref[...]
    cw = cw_ref[...]
    acc = xbuf[pl.ds(8 - (CONV_W - 1), tt), :] * cw[0:1]
    for jj in range(1, CONV_W):
        acc = acc + xbuf[pl.ds(8 - (CONV_W - 1) + jj, tt), :] * cw[jj:jj + 1]
    act = _silu(acc)

    ba = ba_ref[...]
    beta_all = jax.nn.sigmoid(ba)
    sp_in = ba + dtb_ref[...]
    softplus = jnp.maximum(sp_in, 0.0) + jnp.log(1.0 + jnp.exp(-jnp.abs(sp_in)))
    g_all = -jnp.exp(alog_ref[...]) * softplus
    if t_real is not None:
        ridx = t * tt + lax.broadcasted_iota(jnp.int32, (tt, 1), 0)
        live = ridx < t_real
        act = jnp.where(live, act, 0.0)
        beta_all = jnp.where(live, beta_all, 0.0)
        g_all = jnp.where(live, g_all, 0.0)

    nc = tt // c
    nb = nc * DN_H
    ri = lax.broadcasted_iota(jnp.int32, (nb, c, c), 1)
    ci = lax.broadcasted_iota(jnp.int32, (nb, c, c), 2)
    causal, strict = ri >= ci, ri > ci
    eye = jnp.where(ri == ci, 1.0, 0.0).astype(F32)
    ltri = jnp.where(causal, 1.0, 0.0).astype(BF16)

    def stack(arr, off, width):
        return jnp.stack([arr[cc * c:(cc + 1) * c, off + width * h:off + width * (h + 1)]
                          for cc in range(nc) for h in range(DN_H)], axis=0)

    rt = lax.broadcasted_iota(jnp.int32, (tt, tt), 0)
    ct = lax.broadcasted_iota(jnp.int32, (tt, tt), 1)
    chunk_tri = jnp.where((rt >= ct) & (rt // c == ct // c), 1.0, 0.0).astype(BF16)
    gc_all = _dot_exact_lhs(chunk_tri, g_all)

    q = stack(act, 0, DN_DK)
    k = stack(act, DN_QK, DN_DK)
    v = stack(act, 2 * DN_QK, DN_DV)
    q = q * lax.rsqrt(jnp.sum(q * q, axis=-1, keepdims=True) + EPS) * (DN_DK ** -0.5)
    k = k * lax.rsqrt(jnp.sum(k * k, axis=-1, keepdims=True) + EPS)
    beta = stack(beta_all, 0, 1)
    g_col = stack(g_all, DN_H, 1)
    gc = stack(gc_all, DN_H, 1)
    gdiff = _dot_exact_lhs(ltri, jnp.where(strict, g_col, 0.0), _BNN)
    decay = jnp.where(causal, jnp.exp(gdiff), 0.0)
    kb, vb = k * beta, v * beta
    low = jnp.where(strict, _dot(kb, k, hp, _BNT) * decay, 0.0)
    tinv = _tri_inverse(low, eye)
    eg = jnp.exp(gc)
    sol = _dot(tinv, jnp.concatenate([vb, kb * eg], axis=-1), True, _BNN)
    u, w = sol[:, :, :DN_DV], sol[:, :, DN_DV:]
    qk = _dot(q, k, hp, _BNT) * decay
    gc_last = gc[:, c - 1:c, :]
    qd = q * eg
    kd = k * jnp.exp(gc_last - gc)
    gl = jnp.exp(gc_last)
    ng = ng_ref[...]

    s = s_scr[...]
    for cc in range(nc):
        hs = slice(cc * DN_H, (cc + 1) * DN_H)
        v_new = u[hs] - _dot(w[hs], s, hp, _BNN)
        o = _dot(qd[hs], s, hp, _BNN) + _dot(qk[hs], v_new, hp, _BNN)
        kdt = jnp.stack([jnp.transpose(kd[cc * DN_H + h]) for h in range(DN_H)], axis=0)
        s = s * gl[hs] + _dot(kdt, v_new, hp, _BNN)
        for h in range(DN_H):
            zz = z_ref[cc * c:(cc + 1) * c, DN_DV * h:DN_DV * (h + 1)]
            o_ref[cc * c:(cc + 1) * c, DN_DV * h:DN_DV * (h + 1)] = _rms(o[h], ng) * _silu(zz)
    s_scr[...] = s

    @pl.when(t == pl.num_programs(1) - 1)
    def _():
        s_out_ref[...] = s_scr[...]


def _deltanet(qkv, z, ba, conv0, s0, conv_w, a_log, dt_bias, norm_g, *, batch, t_pad, t_real, tt, hp):
    n = qkv.shape[0]
    assert n == batch * t_pad and t_pad % tt == 0 and tt % DN_CHUNK == 0
    nt = t_pad // tt
    conv0p = jnp.pad(conv0, ((0, 0), (8 - (CONV_W - 1), 0), (0, 0)))
    cwp = jnp.pad(conv_w, ((0, 8 - CONV_W), (0, 0)))
    lane_vec = lambda a: jnp.pad(a, (DN_H, LANES - 2 * DN_H)).reshape(1, LANES)
    row = lambda w: pl.BlockSpec((tt, w), lambda b, t: (b * nt + t, 0))
    const = lambda shape: pl.BlockSpec(shape, lambda b, t: (0,) * len(shape))
    a_out, s_out = pl.pallas_call(
        functools.partial(_deltanet_kernel, tt=tt, t_real=None if t_real == t_pad else t_real, hp=hp),
        grid=(batch, nt),
        in_specs=[row(DN_CH), row(DN_QK), row(LANES),
                  pl.BlockSpec((None, 8, DN_CH), lambda b, t: (b, 0, 0)),
                  pl.BlockSpec((None, DN_H, DN_DK, DN_DV), lambda b, t: (b, 0, 0, 0)),
                  const((8, DN_CH)), const((1, LANES)), const((1, LANES)), const((1, DN_DV))],
        out_specs=[row(DN_QK), pl.BlockSpec((None, DN_H, DN_DK, DN_DV), lambda b, t: (b, 0, 0, 0))],
        out_shape=[jax.ShapeDtypeStruct((n, DN_QK), F32), jax.ShapeDtypeStruct(s0.shape, F32)],
        scratch_shapes=[pltpu.VMEM((tt + 8, DN_CH), F32), pltpu.VMEM((DN_H, DN_DK, DN_DV), F32)],
        compiler_params=_cparams(("parallel", "arbitrary")), name="deltanet",
    )(qkv, z, ba, conv0p, s0, cwp, lane_vec(a_log), lane_vec(dt_bias), norm_g.reshape(1, DN_DV))
    return a_out, s_out


def _mix_kernel(x_ref, a_ref, b_ref, ga_ref, gb_ref, wa_ref, wb_ref, wo_ref, fn_ref, *refs, hp, n_experts):
    if n_experts:
        wr_ref, x1_ref, h_ref, comb_ref = refs
    else:
        x1_ref, h_ref = refs
    a = _dot(a_ref[...], wa_ref[...], hp)
    b = _dot(b_ref[...], wb_ref[...], hp)
    mixed = jax.nn.sigmoid(ga_ref[...]) * a + jax.nn.sigmoid(gb_ref[...]) * b
    x1 = x_ref[...] + _dot(mixed, wo_ref[...], hp)
    x1_ref[...] = x1
    h = _rms(x1, fn_ref[...])
    h_ref[...] = h.astype(h_ref.dtype)
    if n_experts:
        logits = _dot(h, wr_ref[...], True)
        lane = lax.broadcasted_iota(jnp.int32, logits.shape, 1).astype(F32)
        lg = jnp.where(lane < n_experts, logits, -jnp.inf)
        m1 = jnp.max(lg, axis=-1, keepdims=True)
        i1 = jnp.min(jnp.where(lg == m1, lane, float(LANES)), axis=-1, keepdims=True)
        lg2 = jnp.where(lane == i1, -jnp.inf, lg)
        m2 = jnp.max(lg2, axis=-1, keepdims=True)
        i2 = jnp.min(jnp.where(lg2 == m2, lane, float(LANES)), axis=-1, keepdims=True)
        e2 = jnp.exp(m2 - m1)
        den = 1.0 + e2
        comb_ref[...] = jnp.where(lane == i1, 1.0 / den, 0.0) + jnp.where(lane == i2, e2 / den, 0.0)


def _mix(x, a_out, b_out, g_raw, wa, wb, wo, fn, w_router, *, hp, tm, h_dtype):
    n, d = x.shape
    assert n % tm == 0
    n_experts = 0 if w_router is None else w_router.shape[1]
    row = lambda w, c=0: pl.BlockSpec((tm, w), lambda i, c=c: (i, c))
    const = lambda shape: pl.BlockSpec(shape, lambda i: (0, 0))
    in_specs = [row(d), row(a_out.shape[1]), row(b_out.shape[1]), row(d, 0), row(d, 1),
                const(wa.shape), const(wb.shape), const(wo.shape), const((1, d))]
    args = [x, a_out, b_out, g_raw, g_raw, wa, wb, wo, fn.reshape(1, d)]
    out_specs = [row(d), row(d)]
    out_shape = [jax.ShapeDtypeStruct((n, d), F32), jax.ShapeDtypeStruct((n, d), h_dtype)]
    if n_experts:
        in_specs.append(const((d, LANES)))
        args.append(jnp.pad(w_router, ((0, 0), (0, LANES - n_experts))))
        out_specs.append(row(LANES))
        out_shape.append(jax.ShapeDtypeStruct((n, LANES), F32))
    return pl.pallas_call(
        functools.partial(_mix_kernel, hp=hp, n_experts=n_experts),
        grid=(n // tm,), in_specs=in_specs, out_specs=out_specs, out_shape=out_shape,
        compiler_params=_cparams(("parallel",)), name="mix_out",
    )(*args)


def _ffn_kernel(x1_ref, h_ref, *refs, hp, use_comb):
    if use_comb:
        comb_ref, wg_ref, wu_ref, wd_ref, o_ref = refs
    else:
        wg_ref, wu_ref, wd_ref, o_ref = refs
    e, f = pl.program_id(1), pl.program_id(2)

    @pl.when((e == 0) & (f == 0))
    def _():
        o_ref[...] = x1_ref[...]

    h = h_ref[...]
    act = _silu(_dot(h, wg_ref[...], hp)) * _dot(h, wu_ref[...], hp)
    y = _dot(act, wd_ref[...], hp)
    if use_comb:
        comb = comb_ref[...]
        lane = lax.broadcasted_iota(jnp.int32, comb.shape, 1)
        y = y * jnp.sum(jnp.where(lane == e, comb, 0.0), axis=-1, keepdims=True)
    o_ref[...] += y


def _ffn(x1, h, comb, w_gu, w_down, *, hp, tm, tf):
    n, d = x1.shape
    n_e, f_dim = w_down.shape[0], w_down.shape[1]
    assert n % tm == 0 and f_dim % tf == 0
    nf = f_dim // tf
    row = lambda w: pl.BlockSpec((tm, w), lambda i, e, f: (i, 0))
    in_specs = [row(d), row(d)]
    args = [x1, h]
    if comb is not None:
        in_specs.append(row(LANES))
        args.append(comb)
    in_specs += [pl.BlockSpec((None, d, tf), lambda i, e, f: (e, 0, f)),
                 pl.BlockSpec((None, d, tf), lambda i, e, f: (e, 0, f + nf)),
                 pl.BlockSpec((None, tf, d), lambda i, e, f: (e, f, 0))]
    args += [w_gu, w_gu, w_down]
    return pl.pallas_call(
        functools.partial(_ffn_kernel, hp=hp, use_comb=comb is not None),
        grid=(n // tm, n_e, nf), in_specs=in_specs, out_specs=row(d),
        out_shape=jax.ShapeDtypeStruct((n, d), F32),
        compiler_params=_cparams(("parallel", "arbitrary", "arbitrary")), name="ffn",
    )(*args)


def _split_w_in(w):
    o = [0]
    for s in (DN_CH, DN_QK, DN_H, DN_H, DA_W, DA_W, DA_W):
        o.append(o[-1] + s)
    ba = jnp.pad(w[:, o[2]:o[4]], ((0, 0), (0, LANES - 2 * DN_H)))
    return [w[:, o[0]:o[1]], w[:, o[1]:o[2]], ba, w[:, o[4]:o[5]], w[:, o[5]:o[6]], w[:, o[6]:o[7]], w[:, o[7]:]]


def _trunk(x, pos0, conv0, ssm0, paged, p, *, hp):
    b, t, d = x.shape
    n = b * t
    depth = p['w_in'].shape[0]
    wdt = F32 if hp else BF16
    cast = lambda w: w.astype(wdt)
    x2 = x.reshape(n, d)
    tm = min(256, n)
    cos, sin = _rope_tables(pos0 + jnp.arange(t, dtype=jnp.int32))
    t_pad = -(-t // DN_CHUNK) * DN_CHUNK
    tt = min(256, t_pad)
    k_rows, v_rows, ssm_out, conv_out = [], [], [], []
    for l in range(depth):
        lam_init = 0.8 - 0.6 * math.exp(-0.3 * l)
        ws = [cast(w) for w in _split_w_in(p['w_in'][l])]
        if hp:
            outs = []
            for grp in (ws[0:3], ws[3:6], ws[6:7]):
                outs += _norm_proj(x2, p['attn_norm'][l], grp, hp=hp, tm=tm)
        else:
            outs = _norm_proj(x2, p['attn_norm'][l], ws, hp=hp, tm=tm)
        qkv_pre, z, ba, q_raw, k_raw, v_raw, g_raw = outs

        q_s, k_rot, k_bf, v_bf = _qk_prep(q_raw, k_raw, v_raw, p['q_norm'][l], p['k_norm'][l], cos, sin,
                                          hp=hp, tm=tm, q_dtype=F32 if hp else BF16)
        lams = (p['lambda_q1'][l], p['lambda_k1'][l], p['lambda_q2'][l], p['lambda_k2'][l])
        if paged is None:
            b_out = _flash(q_s, k_bf, v_bf, lams, p['subln'][l], batch=b, blk=min(512, t), lam_init=lam_init)
        else:
            kpool, vpool, page_table = paged
            b_out = _decode_attn(q_s, k_rot, v_raw, kpool, vpool, page_table, l, lams, p['subln'][l],
                                 lam_init=lam_init, group=math.gcd(16, page_table.shape[1]))

        def pad_t(a):
            if t_pad == t:
                return a
            return jnp.pad(a.reshape(b, t, -1), ((0, 0), (0, t_pad - t), (0, 0))).reshape(b * t_pad, -1)

        a_out, s_new = _deltanet(pad_t(qkv_pre), pad_t(z), pad_t(ba), conv0[l], ssm0[l], p['conv_w'][l],
                                 p['a_log'][l], p['dt_bias'][l], p['dn_norm'][l],
                                 batch=b, t_pad=t_pad, t_real=t, tt=tt, hp=hp)
        if t_pad != t:
            a_out = a_out.reshape(b, t_pad, -1)[:, :t].reshape(n, -1)

        moe = l % 2 == 1
        mixed = _mix(x2, a_out, b_out, g_raw, cast(p['w_a_proj'][l]), cast(p['w_b_proj'][l]), cast(p['w_out'][l]),
                     p['ffn_norm'][l], p['w_router'][l // 2] if moe else None, hp=hp, tm=tm, h_dtype=wdt)
        if moe:
            x1, h2, comb = mixed
            x2 = _ffn(x1, h2, comb, cast(p['w_exp_gu'][l // 2]), cast(p['w_exp_down'][l // 2]),
                      hp=hp, tm=min(512, n), tf=p['w_exp_down'].shape[2])
        else:
            x1, h2 = mixed
            x2 = _ffn(x1, h2, None, cast(p['w_ffn_gu'][l // 2][None]), cast(p['w_ffn_down'][l // 2][None]),
                      hp=hp, tm=min(512, n), tf=p['w_ffn_down'].shape[1] // 2)

        k_rows.append(k_rot.reshape(b, t, 2 * DA_H, DA_DH))
        v_rows.append(v_raw.reshape(b, t, DA_H, 2 * DA_DH))
        ssm_out.append(s_new)
        tail = qkv_pre.reshape(b, t, DN_CH)[:, max(t - (CONV_W - 1), 0):]
        conv_out.append(jnp.concatenate([conv0[l], tail], axis=1)[:, -(CONV_W - 1):])
    return x2.reshape(b, t, d), jnp.stack(k_rows), jnp.stack(v_rows), jnp.stack(ssm_out), jnp.stack(conv_out)


def kernel(x_prompt, x_sample, cache_k, cache_v, state_ssm, state_conv, page_table, attn_norm, w_in, conv_w,
           a_log, dt_bias, dn_norm, q_norm, k_norm, lambda_q1, lambda_k1, lambda_q2, lambda_k2, subln,
           w_a_proj, w_b_proj, w_out, ffn_norm, w_ffn_gu, w_ffn_down, w_router, w_exp_gu, w_exp_down):
    p = dict(attn_norm=attn_norm, w_in=w_in, conv_w=conv_w, a_log=a_log, dt_bias=dt_bias, dn_norm=dn_norm,
             q_norm=q_norm, k_norm=k_norm, lambda_q1=lambda_q1, lambda_k1=lambda_k1, lambda_q2=lambda_q2,
             lambda_k2=lambda_k2, subln=subln, w_a_proj=w_a_proj, w_b_proj=w_b_proj, w_out=w_out,
             ffn_norm=ffn_norm, w_ffn_gu=w_ffn_gu, w_ffn_down=w_ffn_down, w_router=w_router,
             w_exp_gu=w_exp_gu, w_exp_down=w_exp_down)
    depth = w_in.shape[0]
    bp = x_prompt.shape[0]
    assert x_sample.shape[1] == 1, "the decode path handles one new token per sequence"
    assert cache_k.shape[2] == PAGE
    conv0 = jnp.zeros((depth, bp, CONV_W - 1, DN_CH), F32)
    ssm0 = jnp.zeros((depth, bp, DN_H, DN_DK, DN_DV), F32)
    y_p, k_p, v_p, ssm_p, conv_p = _trunk(x_prompt, 0, conv0, ssm0, None, p, hp=False)

    past_len = page_table.shape[1] * PAGE
    n_pool = cache_k.shape[1]
    kpool = jnp.transpose(cache_k, (0, 1, 3, 4, 2))
    vpool = cache_v.reshape(depth, n_pool, PAGE * DA_H, 2 * DA_DH)
    y_s, k_s, v_s, ssm_s, conv_s = _trunk(x_sample, past_len, state_conv, state_ssm, (kpool, vpool, page_table), p, hp=True)
    return (y_p, y_s, k_p, v_p, ssm_p, conv_p, k_s, v_s, ssm_s, conv_s)
```

```python
import functools
import math

import jax
import jax.numpy as jnp
from jax import lax
from jax.experimental import pallas as pl
from jax.experimental.pallas import tpu as pltpu

F32 = jnp.float32
BF16 = jnp.bfloat16
EPS = 1e-6
LANES = 128
VMEM_LIMIT = 56 * 1024 * 1024

DN_H, DN_DK, DN_DV = 4, 128, 128
DN_QK = DN_H * DN_DK
DN_CH = 3 * DN_QK
CONV_W = 4
DN_CHUNK = 64
DA_H, DA_DH = 4, 64
DA_W = 2 * DA_H * DA_DH
ROPE_THETA = 10000.0
PAGE = 128

_NN = (((1,), (0,)), ((), ()))
_NT = (((1,), (1,)), ((), ()))
_BNN = (((2,), (1,)), ((0,), (0,)))
_BNT = (((2,), (2,)), ((0,), (0,)))
assert 2 * DA_DH == LANES and DN_DV == LANES


def _cparams(sem):
    return pltpu.CompilerParams(dimension_semantics=sem, vmem_limit_bytes=VMEM_LIMIT)


def _mm(a, b, dims=_NN):
    return lax.dot_general(a, b, dims, preferred_element_type=F32)


def _split(a):
    if a.dtype == BF16:
        return a, None
    hi = a.astype(BF16)
    lo = (a - hi.astype(F32)).astype(BF16)
    return hi, lo


def _dot(a, b, hp, dims=_NN):
    if not hp:
        return _mm(a.astype(BF16), b.astype(BF16), dims)
    ah, al = _split(a)
    bh, bl = _split(b)
    out = _mm(ah, bh, dims)
    if al is not None:
        out = out + _mm(al, bh, dims)
    if bl is not None:
        out = out + _mm(ah, bl, dims)
    return out


def _split3(a):
    hi = a.astype(BF16)
    r = a - hi.astype(F32)
    mid = r.astype(BF16)
    lo = (r - mid.astype(F32)).astype(BF16)
    return hi, mid, lo


def _dot_exact_lhs(a_bf16, b, dims=_NN):
    hi, mid, lo = _split3(b)
    return _mm(a_bf16, hi, dims) + (_mm(a_bf16, mid, dims) + _mm(a_bf16, lo, dims))


def _dot_exact_rhs(a, b_bf16):
    hi, mid, lo = _split3(a)
    return _mm(hi, b_bf16) + (_mm(mid, b_bf16) + _mm(lo, b_bf16))


def _rms(x, g):
    return x * lax.rsqrt(jnp.mean(x * x, axis=-1, keepdims=True) + EPS) * g


def _silu(x):
    return x * jax.nn.sigmoid(x)


def _norm_proj_kernel(x_ref, g_ref, *refs, n_w, hp):
    w_refs, o_refs = refs[:n_w], refs[n_w:]
    h = _rms(x_ref[...], g_ref[...])
    if not hp:
        h = h.astype(BF16)
    for w_ref, o_ref in zip(w_refs, o_refs):
        o_ref[...] = _dot(h, w_ref[...], hp).astype(o_ref.dtype)


def _norm_proj(x, g, ws, *, hp, tm):
    n, d = x.shape
    assert n % tm == 0
    in_specs = [pl.BlockSpec((tm, d), lambda i: (i, 0)), pl.BlockSpec((1, d), lambda i: (0, 0))]
    in_specs += [pl.BlockSpec(w.shape, lambda i: (0, 0)) for w in ws]
    out_specs = [pl.BlockSpec((tm, w.shape[1]), lambda i: (i, 0)) for w in ws]
    out_shape = [jax.ShapeDtypeStruct((n, w.shape[1]), F32) for w in ws]
    return pl.pallas_call(
        functools.partial(_norm_proj_kernel, n_w=len(ws), hp=hp),
        grid=(n // tm,), in_specs=in_specs, out_specs=out_specs, out_shape=out_shape,
        compiler_params=_cparams(("parallel",)), name="norm_proj",
    )(x, g.reshape(1, d), *ws)


def _qk_prep_kernel(q_ref, k_ref, v_ref, qn_ref, kn_ref, cos_ref, sin_ref, bd_ref,
                    qo_ref, ko_ref, kb_ref, vb_ref, *, hp, q_scale):
    tm = q_ref.shape[0]
    lane = lax.broadcasted_iota(jnp.int32, (tm, DA_W), 1)
    first_half = (lane % DA_DH) < (DA_DH // 2)
    bd = bd_ref[...]
    cos, sin = cos_ref[...], sin_ref[...]
    q = _norm_rope(q_ref[...], qn_ref[...], bd, cos, sin, first_half, hp) * q_scale
    k = _norm_rope(k_ref[...], kn_ref[...], bd, cos, sin, first_half, hp)
    qo_ref[...] = q.astype(qo_ref.dtype)
    ko_ref[...] = k
    kb_ref[...] = k.astype(BF16)
    vb_ref[...] = v_ref[...].astype(BF16)


def _rope_tables(pos):
    half = DA_DH // 2
    inv = ROPE_THETA ** (-jnp.arange(half, dtype=F32) / half)
    ang = pos.astype(F32)[:, None] * inv[None, :]
    cos, sin = jnp.cos(ang), jnp.sin(ang)
    cos = jnp.tile(jnp.concatenate([cos, cos], axis=-1), (1, 2 * DA_H))
    sin = jnp.tile(jnp.concatenate([-sin, sin], axis=-1), (1, 2 * DA_H))
    return cos, sin


def _qk_prep(q_raw, k_raw, v_raw, qn, kn, cos, sin, *, hp, tm, q_dtype):
    n = q_raw.shape[0]
    t = cos.shape[0]
    if t % tm:
        cos, sin = jnp.tile(cos, (n // t, 1)), jnp.tile(sin, (n // t, 1))
        t = n
    assert n % tm == 0 and t % tm == 0
    nt = t // tm
    idx = jnp.arange(DA_W) // DA_DH
    bd = (idx[:, None] == idx[None, :]).astype(BF16)
    row = pl.BlockSpec((tm, DA_W), lambda i: (i, 0))
    vec = pl.BlockSpec((1, DA_W), lambda i: (0, 0))
    tab = pl.BlockSpec((tm, DA_W), lambda i: (i % nt, 0))
    return pl.pallas_call(
        functools.partial(_qk_prep_kernel, hp=hp, q_scale=DA_DH ** -0.5),
        grid=(n // tm,),
        in_specs=[row, row, row, vec, vec, tab, tab, pl.BlockSpec((DA_W, DA_W), lambda i: (0, 0))],
        out_specs=[row, row, row, row],
        out_shape=[jax.ShapeDtypeStruct((n, DA_W), q_dtype), jax.ShapeDtypeStruct((n, DA_W), F32),
                   jax.ShapeDtypeStruct((n, DA_W), BF16), jax.ShapeDtypeStruct((n, DA_W), BF16)],
        compiler_params=_cparams(("parallel",)), name="qk_prep",
    )(q_raw, k_raw, v_raw, jnp.tile(qn, 2 * DA_H).reshape(1, DA_W), jnp.tile(kn, 2 * DA_H).reshape(1, DA_W), cos, sin, bd)


def _norm_rope(x, g, bd, cos, sin, first_half, hp):
    sq = x * x
    ss = _dot_exact_rhs(sq, bd) if hp else _mm(sq.astype(BF16), bd)
    y = x * lax.rsqrt(ss * (1.0 / DA_DH) + EPS) * g
    partner = jnp.where(first_half, pltpu.roll(y, DA_W - DA_DH // 2, axis=1), pltpu.roll(y, DA_DH // 2, axis=1))
    return y * cos + partner * sin


def _in_proj_kernel(x_ref, g_ref, wqkv, wz, wba, wq, wk, wv, wg, qn_ref, kn_ref, cos_ref, sin_ref, bd_ref, *refs):
    qkv_o, z_o, ba_o, g_o, q_o, kb_o, vb_o, kt_o, v4_o = refs[-9:]
    tm = x_ref.shape[0]
    h = _rms(x_ref[...], g_ref[...]).astype(BF16)
    qkv_o[...] = _mm(h, wqkv[...])
    z_o[...] = _mm(h, wz[...])
    ba_o[...] = _mm(h, wba[...])
    g_o[...] = _mm(h, wg[...])
    lane = lax.broadcasted_iota(jnp.int32, (tm, DA_W), 1)
    first_half = (lane % DA_DH) < (DA_DH // 2)
    bd, cos, sin = bd_ref[...], cos_ref[...], sin_ref[...]
    q = _norm_rope(_mm(h, wq[...]), qn_ref[...], bd, cos, sin, first_half, False) * (DA_DH ** -0.5)
    k = _norm_rope(_mm(h, wk[...]), kn_ref[...], bd, cos, sin, first_half, False)
    v = _mm(h, wv[...])
    q_o[...] = q.astype(BF16)
    kb_o[...] = k.astype(BF16)
    vb_o[...] = v.astype(BF16)
    kt_o[...] = jnp.transpose(k)
    for hh in range(DA_H):
        v4_o[pl.ds(hh, tm, stride=DA_H), :] = v[:, 2 * DA_DH * hh:2 * DA_DH * (hh + 1)]


def _in_proj(x, g, ws, qn, kn, cos, sin, kv_rows, *, batch, tm, layer, depth):
    n, d = x.shape
    t = n // batch
    assert n % tm == 0 and t % tm == 0
    nt = t // tm
    idx = jnp.arange(DA_W) // DA_DH
    bd = (idx[:, None] == idx[None, :]).astype(BF16)
    row = lambda w, dt=F32: (pl.BlockSpec((tm, w), lambda i: (i, 0)), jax.ShapeDtypeStruct((n, w), dt))
    const = lambda shape: pl.BlockSpec(shape, lambda i: (0, 0))
    tab = pl.BlockSpec((tm, DA_W), lambda i: (i % nt, 0))
    outs = [row(DN_CH), row(DN_QK), row(LANES), row(ws[6].shape[1]), row(DA_W, BF16), row(DA_W, BF16), row(DA_W, BF16),
            (pl.BlockSpec((None, None, DA_W, tm), lambda i: (layer, i // nt, 0, i % nt)),
             jax.ShapeDtypeStruct((depth, batch, DA_W, t), F32)),
            (pl.BlockSpec((None, tm * DA_H, 2 * DA_DH), lambda i: (layer, i, 0)),
             jax.ShapeDtypeStruct((depth, n * DA_H, 2 * DA_DH), F32))]
    in_specs = [pl.BlockSpec((tm, d), lambda i: (i, 0)), const((1, d))] + [const(w.shape) for w in ws]
    in_specs += [const((1, DA_W)), const((1, DA_W)), tab, tab, const((DA_W, DA_W))]
    args = [x, g.reshape(1, d), *ws, jnp.tile(qn, 2 * DA_H).reshape(1, DA_W), jnp.tile(kn, 2 * DA_H).reshape(1, DA_W),
            cos, sin, bd]
    aliases = {}
    if kv_rows is not None:
        aliases = {len(args): len(outs) - 2, len(args) + 1: len(outs) - 1}
        in_specs += [pl.BlockSpec(memory_space=pl.ANY)] * 2
        args += list(kv_rows)
    return pl.pallas_call(
        _in_proj_kernel, grid=(n // tm,), in_specs=in_specs,
        out_specs=[o[0] for o in outs], out_shape=[o[1] for o in outs], input_output_aliases=aliases,
        compiler_params=_cparams(("parallel",)), name="in_proj",
    )(*args)


def _lambda(lq1, lk1, lq2, lk2, lam_init):
    return (jnp.exp(jnp.sum(lq1[...] * lk1[...], axis=-1, keepdims=True))
            - jnp.exp(jnp.sum(lq2[...] * lk2[...], axis=-1, keepdims=True)) + lam_init)


def _flash_kernel(qi_ref, kj_ref, q_ref, k_ref, v_ref, lq1, lk1, lq2, lk2, sg_ref, o_ref, qs_scr, m_scr, l_scr,
                  acc_scr, s_scr, p_scr, al_scr, *, blk, lam_init):
    i, j = qi_ref[pl.program_id(1)], kj_ref[pl.program_id(1)]
    hw = 2 * DA_DH
    rb = min(64, blk)

    @pl.when(j == 0)
    def _():
        q = q_ref[...]
        lane = lax.broadcasted_iota(jnp.int32, (blk, hw), 1)
        zero = jnp.zeros((blk, hw), q.dtype)
        for h in range(DA_H):
            qh = q[:, hw * h:hw * (h + 1)]
            qs_scr[h, 0:blk, :] = jnp.where(lane < DA_DH, qh, zero)
            qs_scr[h, blk:2 * blk, :] = jnp.where(lane >= DA_DH, qh, zero)
        m_scr[...] = jnp.full(m_scr.shape, -jnp.inf, F32)
        l_scr[...] = jnp.zeros(l_scr.shape, F32)
        acc_scr[...] = jnp.zeros(acc_scr.shape, F32)

    def step(diag):
        if diag:
            r = lax.broadcasted_iota(jnp.int32, (2 * blk, blk), 0)
            c = lax.broadcasted_iota(jnp.int32, (2 * blk, blk), 1)
            keep = c <= jnp.where(r >= blk, r - blk, r)
        for h in range(DA_H):
            s = _mm(qs_scr[h], k_ref[:, hw * h:hw * (h + 1)], _NT)
            if diag:
                s = jnp.where(keep, s, -jnp.inf)
            s_scr[h] = s
            m_prev = m_scr[h]
            m_new = jnp.maximum(m_prev, jnp.max(s, axis=-1, keepdims=True))
            m_scr[h] = m_new
            al_scr[h] = jnp.exp(m_prev - m_new)
            for t in range(2 * blk // rb):
                rows = slice(t * rb, (t + 1) * rb)
                m_rows = m_scr[h, rows, :]
                p = jnp.exp(s_scr[h, rows, :] - jnp.concatenate([m_rows] * (blk // LANES), axis=1))
                l_scr[h, rows, :] = al_scr[h, rows, :] * l_scr[h, rows, :] + jnp.sum(p, axis=-1, keepdims=True)
                p_scr[h, rows, :] = p.astype(BF16)
            acc_scr[h] = al_scr[h] * acc_scr[h] + _mm(p_scr[h], v_ref[:, hw * h:hw * (h + 1)])

    @pl.when(j < i)
    def _():
        step(False)

    @pl.when(j == i)
    def _():
        step(True)
        lam = _lambda(lq1, lk1, lq2, lk2, lam_init)
        for h in range(DA_H):
            acc = acc_scr[h]
            l = l_scr[h]
            o = acc[0:blk] / l[0:blk] - lam * (acc[blk:] / l[blk:])
            o_ref[:, hw * h:hw * (h + 1)] = _rms(o, sg_ref[...]) * (1.0 - lam_init)


def _flash(q, k, v, lams, subln, *, batch, blk, lam_init):
    n = q.shape[0]
    nq = n // batch // blk
    hw = 2 * DA_DH
    pairs = [(i, j) for i in range(nq) for j in range(i + 1)]
    qi = jnp.array([ij[0] for ij in pairs], jnp.int32)
    kj = jnp.array([ij[1] for ij in pairs], jnp.int32)
    qspec = pl.BlockSpec((blk, DA_W), lambda b, s, qi, kj: (b * nq + qi[s], 0))
    kspec = pl.BlockSpec((blk, DA_W), lambda b, s, qi, kj: (b * nq + kj[s], 0))
    lspec = pl.BlockSpec((1, DA_DH), lambda b, s, qi, kj: (0, 0))
    return pl.pallas_call(
        functools.partial(_flash_kernel, blk=blk, lam_init=lam_init),
        grid_spec=pltpu.PrefetchScalarGridSpec(
            num_scalar_prefetch=2, grid=(batch, len(pairs)),
            in_specs=[qspec, kspec, kspec, lspec, lspec, lspec, lspec,
                      pl.BlockSpec((1, hw), lambda b, s, qi, kj: (0, 0))],
            out_specs=qspec,
            scratch_shapes=[pltpu.VMEM((DA_H, 2 * blk, hw), BF16), pltpu.VMEM((DA_H, 2 * blk, LANES), F32),
                            pltpu.VMEM((DA_H, 2 * blk, LANES), F32), pltpu.VMEM((DA_H, 2 * blk, hw), F32),
                            pltpu.VMEM((DA_H, 2 * blk, blk), F32), pltpu.VMEM((DA_H, 2 * blk, blk), BF16),
                            pltpu.VMEM((DA_H, 2 * blk, LANES), F32)]),
        out_shape=jax.ShapeDtypeStruct((n, DA_W), F32),
        compiler_params=_cparams(("parallel", "arbitrary")), name="flash_diff_attn",
    )(qi, kj, q, k, v, *[x.reshape(1, DA_DH) for x in lams], subln.reshape(1, hw))


def _decode_kernel(pt_ref, qbd_ref, q8_ref, kn_ref, vn_ref, lq1, lk1, lq2, lk2, sg_ref, *refs,
                   group, lam_init):
    k_refs, v_refs = refs[:group], refs[group:2 * group]
    o_ref, m_scr, l_scr, acc_scr = refs[2 * group:]
    s_idx = pl.program_id(1)
    n_maps = 2 * DA_H

    @pl.when(s_idx == 0)
    def _():
        s0 = jnp.sum(q8_ref[...] * kn_ref[...], axis=-1, keepdims=True)
        m_scr[...] = jnp.broadcast_to(s0, (n_maps, LANES))
        l_scr[...] = jnp.ones((n_maps, LANES), F32)
        for h in range(DA_H):
            acc_scr[h] = jnp.broadcast_to(vn_ref[h:h + 1, :], (n_maps, 2 * DA_DH))

    q_hi, q_lo = _split(qbd_ref[...])
    a16 = jnp.concatenate([q_hi, q_lo], axis=0)

    s_list = []
    for g in range(group):
        kt = k_refs[g][...].reshape(n_maps * DA_DH, PAGE)
        k_hi, k_lo = _split(kt)
        r = _mm(a16, k_hi) + _mm(a16, k_lo)
        s_list.append(r[0:n_maps] + r[n_maps:])
    m_prev = m_scr[...][:, :1]
    m_cur = s_list[0].max(axis=-1, keepdims=True)
    for s in s_list[1:]:
        m_cur = jnp.maximum(m_cur, s.max(axis=-1, keepdims=True))
    m_new = jnp.maximum(m_prev, m_cur)
    alpha = jnp.exp(m_prev - m_new)
    l_new = alpha * l_scr[...][:, :1]
    acc = [alpha * acc_scr[h] for h in range(DA_H)]
    for g in range(group):
        p = jnp.exp(s_list[g] - m_new)
        l_new = l_new + jnp.sum(p, axis=-1, keepdims=True)
        p_hi, p_lo = _split(p)
        p16 = jnp.concatenate([p_hi, p_lo], axis=0)
        for h in range(DA_H):
            v_hi, v_lo = _split(v_refs[g][pl.ds(h, PAGE, stride=DA_H), :])
            r = _mm(p16, v_hi) + _mm(p16, v_lo)
            acc[h] = acc[h] + (r[0:n_maps] + r[n_maps:])
    m_scr[...] = jnp.broadcast_to(m_new, (n_maps, LANES))
    l_scr[...] = jnp.broadcast_to(l_new, (n_maps, LANES))
    for h in range(DA_H):
        acc_scr[h] = acc[h]

    @pl.when(s_idx == pl.num_programs(1) - 1)
    def _():
        lam = _lambda(lq1, lk1, lq2, lk2, lam_init)
        for h in range(DA_H):
            o1 = acc[h][2 * h:2 * h + 1] / l_new[2 * h:2 * h + 1]
            o2 = acc[h][2 * h + 1:2 * h + 2] / l_new[2 * h + 1:2 * h + 2]
            o_ref[h:h + 1, :] = _rms(o1 - lam * o2, sg_ref[...]) * (1.0 - lam_init)


def _decode_attn(q, k_new, v_new, kpool, vpool, page_table, layer, lams, subln, *, lam_init, group):
    s, n_pages = page_table.shape
    assert n_pages % group == 0
    hw = 2 * DA_DH
    n_maps = 2 * DA_H
    q8 = q.reshape(s, n_maps, DA_DH)
    qbd = (q8[:, :, None, :] * jnp.eye(n_maps, dtype=F32)[None, :, :, None]).reshape(s, n_maps, DA_W)
    kn8 = k_new.reshape(s, n_maps, DA_DH)
    vn4 = v_new.reshape(s, DA_H, hw)

    small = lambda shape: pl.BlockSpec((None,) + shape, lambda b, t, pt: (b, 0, 0))
    lspec = pl.BlockSpec((1, DA_DH), lambda b, t, pt: (0, 0))
    in_specs = [small((n_maps, DA_W)), small((n_maps, DA_DH)), small((n_maps, DA_DH)), small((DA_H, hw)),
                lspec, lspec, lspec, lspec, pl.BlockSpec((1, hw), lambda b, t, pt: (0, 0))]
    in_specs += [pl.BlockSpec((None, None, n_maps, DA_DH, PAGE),
                              lambda b, t, pt, g=g: (layer, pt[b, t * group + g], 0, 0, 0)) for g in range(group)]
    in_specs += [pl.BlockSpec((None, None, PAGE * DA_H, hw),
                              lambda b, t, pt, g=g: (layer, pt[b, t * group + g], 0, 0)) for g in range(group)]
    out = pl.pallas_call(
        functools.partial(_decode_kernel, group=group, lam_init=lam_init),
        grid_spec=pltpu.PrefetchScalarGridSpec(
            num_scalar_prefetch=1, grid=(s, n_pages // group), in_specs=in_specs,
            out_specs=small((DA_H, hw)),
            scratch_shapes=[pltpu.VMEM((n_maps, LANES), F32), pltpu.VMEM((n_maps, LANES), F32),
                            pltpu.VMEM((DA_H, n_maps, hw), F32)]),
        out_shape=jax.ShapeDtypeStruct((s, DA_H, hw), F32),
        compiler_params=_cparams(("parallel", "arbitrary")), name="decode_diff_attn",
    )(page_table, qbd, q8, kn8, vn4, *[x.reshape(1, DA_DH) for x in lams], subln.reshape(1, hw),
      *([kpool] * group), *([vpool] * group))
    return out.reshape(s, DA_W)


def _tri_inverse(low, eye):
    a = -low
    t = eye + a
    p = a
    steps = int(math.log2(low.shape[-1])) - 1
    for _ in range(steps):
        p = _dot(p, p, True, _BNN)
        t = t + _dot(t, p, True, _BNN)
    return t


def _deltanet_kernel(qkv_ref, z_ref, ba_ref, conv0_ref, s0_ref, cw_ref, alog_ref, dtb_ref, ng_ref,
                     o_ref, s_out_ref, xbuf, s_scr, *, tt, t_real, hp):
    t = pl.program_id(1)
    c = DN_CHUNK

    @pl.when(t == 0)
    def _():
        xbuf[0:8, :] = conv0_ref[...]
        s_scr[...] = s0_ref[...]

    @pl.when(t > 0)
    def _():
        xbuf[0:8, :] = xbuf[tt:tt + 8, :]

    xbuf[8:8 + tt, :] = qkv_ref[...]
    cw = cw_ref[...]
    acc = xbuf[pl.ds(8 - (CONV_W - 1), tt), :] * cw[0:1]
    for jj in range(1, CONV_W):
        acc = acc + xbuf[pl.ds(8 - (CONV_W - 1) + jj, tt), :] * cw[jj:jj + 1]
    act = _silu(acc)

    ba = ba_ref[...]
    beta_all = jax.nn.sigmoid(ba)
    sp_in = ba + dtb_ref[...]
    softplus = jnp.maximum(sp_in, 0.0) + jnp.log(1.0 + jnp.exp(-jnp.abs(sp_in)))
    g_all = -jnp.exp(alog_ref[...]) * softplus
    if t_real is not None:
        ridx = t * tt + lax.broadcasted_iota(jnp.int32, (tt, 1), 0)
        live = ridx < t_real
        act = jnp.where(live, act, 0.0)
        beta_all = jnp.where(live, beta_all, 0.0)
        g_all = jnp.where(live, g_all, 0.0)

    nc = tt // c
    nb = nc * DN_H
    ri = lax.broadcasted_iota(jnp.int32, (nb, c, c), 1)
    ci = lax.broadcasted_iota(jnp.int32, (nb, c, c), 2)
    causal, strict = ri >= ci, ri > ci
    eye = jnp.where(ri == ci, 1.0, 0.0).astype(F32)
    ltri = jnp.where(causal, 1.0, 0.0).astype(BF16)

    def stack(arr, off, width):
        return jnp.stack([arr[cc * c:(cc + 1) * c, off + width * h:off + width * (h + 1)]
                          for cc in range(nc) for h in range(DN_H)], axis=0)

    rt = lax.broadcasted_iota(jnp.int32, (tt, tt), 0)
    ct = lax.broadcasted_iota(jnp.int32, (tt, tt), 1)
    chunk_tri = jnp.where((rt >= ct) & (rt // c == ct // c), 1.0, 0.0).astype(BF16)
    gc_all = _dot_exact_lhs(chunk_tri, g_all)

    q = stack(act, 0, DN_DK)
    k = stack(act, DN_QK, DN_DK)
    v = stack(act, 2 * DN_QK, DN_DV)
    q = q * lax.rsqrt(jnp.sum(q * q, axis=-1, keepdims=True) + EPS) * (DN_DK ** -0.5)
    k = k * lax.rsqrt(jnp.sum(k * k, axis=-1, keepdims=True) + EPS)
    beta = stack(beta_all, 0, 1)
    g_col = stack(g_all, DN_H, 1)
    gc = stack(gc_all, DN_H, 1)
    gdiff = _dot_exact_lhs(ltri, jnp.where(strict, g_col, 0.0), _BNN)
    decay = jnp.where(causal, jnp.exp(gdiff), 0.0)
    kb, vb = k * beta, v * beta
    low = jnp.where(strict, _dot(kb, k, hp, _BNT) * decay, 0.0)
    tinv = _tri_inverse(low, eye)
    eg = jnp.exp(gc)
    sol = _dot(tinv, jnp.concatenate([vb, kb * eg], axis=-1), True, _BNN)
    u, w = sol[:, :, :DN_DV], sol[:, :, DN_DV:]
    qk = _dot(q, k, hp, _BNT) * decay
    gc_last = gc[:, c - 1:c, :]
    qd = q * eg
    kd = k * jnp.exp(gc_last - gc)
    gl = jnp.exp(gc_last)
    ng = ng_ref[...]

    s = s_scr[...]
    for cc in range(nc):
        hs = slice(cc * DN_H, (cc + 1) * DN_H)
        v_new = u[hs] - _dot(w[hs], s, hp, _BNN)
        o = _dot(qd[hs], s, hp, _BNN) + _dot(qk[hs], v_new, hp, _BNN)
        kdt = jnp.stack([jnp.transpose(kd[cc * DN_H + h]) for h in range(DN_H)], axis=0)
        s = s * gl[hs] + _dot(kdt, v_new, hp, _BNN)
        for h in range(DN_H):
            zz = z_ref[cc * c:(cc + 1) * c, DN_DV * h:DN_DV * (h + 1)]
            o_ref[cc * c:(cc + 1) * c, DN_DV * h:DN_DV * (h + 1)] = _rms(o[h], ng) * _silu(zz)
    s_scr[...] = s

    @pl.when(t == pl.num_programs(1) - 1)
    def _():
        s_out_ref[...] = s_scr[...]


def _deltanet(qkv, z, ba, conv0, s0, conv_w, a_log, dt_bias, norm_g, *, batch, t_pad, t_real, tt, hp):
    n = qkv.shape[0]
    assert n == batch * t_pad and t_pad % tt == 0 and tt % DN_CHUNK == 0
    nt = t_pad // tt
    conv0p = jnp.pad(conv0, ((0, 0), (8 - (CONV_W - 1), 0), (0, 0)))
    cwp = jnp.pad(conv_w, ((0, 8 - CONV_W), (0, 0)))
    lane_vec = lambda a: jnp.pad(a, (DN_H, LANES - 2 * DN_H)).reshape(1, LANES)
    row = lambda w: pl.BlockSpec((tt, w), lambda b, t: (b * nt + t, 0))
    const = lambda shape: pl.BlockSpec(shape, lambda b, t: (0,) * len(shape))
    a_out, s_out = pl.pallas_call(
        functools.partial(_deltanet_kernel, tt=tt, t_real=None if t_real == t_pad else t_real, hp=hp),
        grid=(batch, nt),
        in_specs=[row(DN_CH), row(DN_QK), row(LANES),
                  pl.BlockSpec((None, 8, DN_CH), lambda b, t: (b, 0, 0)),
                  pl.BlockSpec((None, DN_H, DN_DK, DN_DV), lambda b, t: (b, 0, 0, 0)),
                  const((8, DN_CH)), const((1, LANES)), const((1, LANES)), const((1, DN_DV))],
        out_specs=[row(DN_QK), pl.BlockSpec((None, DN_H, DN_DK, DN_DV), lambda b, t: (b, 0, 0, 0))],
        out_shape=[jax.ShapeDtypeStruct((n, DN_QK), F32), jax.ShapeDtypeStruct(s0.shape, F32)],
        scratch_shapes=[pltpu.VMEM((tt + 8, DN_CH), F32), pltpu.VMEM((DN_H, DN_DK, DN_DV), F32)],
        compiler_params=_cparams(("parallel", "arbitrary")), name="deltanet",
    )(qkv, z, ba, conv0p, s0, cwp, lane_vec(a_log), lane_vec(dt_bias), norm_g.reshape(1, DN_DV))
    return a_out, s_out


def _mix_kernel(x_ref, a_ref, b_ref, ga_ref, gb_ref, wa_ref, wb_ref, wo_ref, fn_ref, *refs, hp, n_experts):
    if n_experts:
        wr_ref, x1_ref, h_ref, comb_ref = refs
    else:
        x1_ref, h_ref = refs
    a = _dot(a_ref[...], wa_ref[...], hp)
    b = _dot(b_ref[...], wb_ref[...], hp)
    mixed = jax.nn.sigmoid(ga_ref[...]) * a + jax.nn.sigmoid(gb_ref[...]) * b
    x1 = x_ref[...] + _dot(mixed, wo_ref[...], hp)
    x1_ref[...] = x1
    h = _rms(x1, fn_ref[...])
    h_ref[...] = h.astype(h_ref.dtype)
    if n_experts:
        logits = _dot(h, wr_ref[...], True)
        lane = lax.broadcasted_iota(jnp.int32, logits.shape, 1).astype(F32)
        lg = jnp.where(lane < n_experts, logits, -jnp.inf)
        m1 = jnp.max(lg, axis=-1, keepdims=True)
        i1 = jnp.min(jnp.where(lg == m1, lane, float(LANES)), axis=-1, keepdims=True)
        lg2 = jnp.where(lane == i1, -jnp.inf, lg)
        m2 = jnp.max(lg2, axis=-1, keepdims=True)
        i2 = jnp.min(jnp.where(lg2 == m2, lane, float(LANES)), axis=-1, keepdims=True)
        e2 = jnp.exp(m2 - m1)
        den = 1.0 + e2
        comb_ref[...] = jnp.where(lane == i1, 1.0 / den, 0.0) + jnp.where(lane == i2, e2 / den, 0.0)


def _mix(x, a_out, b_out, g_raw, wa, wb, wo, fn, w_router, *, hp, tm, h_dtype):
    n, d = x.shape
    assert n % tm == 0
    n_experts = 0 if w_router is None else w_router.shape[1]
    row = lambda w, c=0: pl.BlockSpec((tm, w), lambda i, c=c: (i, c))
    const = lambda shape: pl.BlockSpec(shape, lambda i: (0, 0))
    in_specs = [row(d), row(a_out.shape[1]), row(b_out.shape[1]), row(d, 0), row(d, 1),
                const(wa.shape), const(wb.shape), const(wo.shape), const((1, d))]
    args = [x, a_out, b_out, g_raw, g_raw, wa, wb, wo, fn.reshape(1, d)]
    out_specs = [row(d), row(d)]
    out_shape = [jax.ShapeDtypeStruct((n, d), F32), jax.ShapeDtypeStruct((n, d), h_dtype)]
    if n_experts:
        in_specs.append(const((d, LANES)))
        args.append(jnp.pad(w_router, ((0, 0), (0, LANES - n_experts))))
        out_specs.append(row(LANES))
        out_shape.append(jax.ShapeDtypeStruct((n, LANES), F32))
    return pl.pallas_call(
        functools.partial(_mix_kernel, hp=hp, n_experts=n_experts),
        grid=(n // tm,), in_specs=in_specs, out_specs=out_specs, out_shape=out_shape,
        compiler_params=_cparams(("parallel",)), name="mix_out",
    )(*args)


def _ffn_kernel(x1_ref, h_ref, *refs, hp, use_comb):
    if use_comb:
        comb_ref, wg_ref, wu_ref, wd_ref, o_ref = refs
    else:
        wg_ref, wu_ref, wd_ref, o_ref = refs
    e, f = pl.program_id(1), pl.program_id(2)

    @pl.when((e == 0) & (f == 0))
    def _():
        o_ref[...] = x1_ref[...]

    h = h_ref[...]
    act = _silu(_dot(h, wg_ref[...], hp)) * _dot(h, wu_ref[...], hp)
    y = _dot(act, wd_ref[...], hp)
    if use_comb:
        comb = comb_ref[...]
        lane = lax.broadcasted_iota(jnp.int32, comb.shape, 1)
        y = y * jnp.sum(jnp.where(lane == e, comb, 0.0), axis=-1, keepdims=True)
    o_ref[...] += y


def _ffn(x1, h, comb, w_gu, w_down, *, hp, tm, tf):
    n, d = x1.shape
    n_e, f_dim = w_down.shape[0], w_down.shape[1]
    assert n % tm == 0 and f_dim % tf == 0
    nf = f_dim // tf
    row = lambda w: pl.BlockSpec((tm, w), lambda i, e, f: (i, 0))
    in_specs = [row(d), row(d)]
    args = [x1, h]
    if comb is not None:
        in_specs.append(row(LANES))
        args.append(comb)
    in_specs += [pl.BlockSpec((None, d, tf), lambda i, e, f: (e, 0, f)),
                 pl.BlockSpec((None, d, tf), lambda i, e, f: (e, 0, f + nf)),
                 pl.BlockSpec((None, tf, d), lambda i, e, f: (e, f, 0))]
    args += [w_gu, w_gu, w_down]
    return pl.pallas_call(
        functools.partial(_ffn_kernel, hp=hp, use_comb=comb is not None),
        grid=(n // tm, n_e, nf), in_specs=in_specs, out_specs=row(d),
        out_shape=jax.ShapeDtypeStruct((n, d), F32),
        compiler_params=_cparams(("parallel", "arbitrary", "arbitrary")), name="ffn",
    )(*args)


def _split_w_in(w):
    o = [0]
    for s in (DN_CH, DN_QK, DN_H, DN_H, DA_W, DA_W, DA_W):
        o.append(o[-1] + s)
    ba = jnp.pad(w[:, o[2]:o[4]], ((0, 0), (0, LANES - 2 * DN_H)))
    return [w[:, o[0]:o[1]], w[:, o[1]:o[2]], ba, w[:, o[4]:o[5]], w[:, o[5]:o[6]], w[:, o[6]:o[7]], w[:, o[7]:]]


def _trunk(x, pos0, conv0, ssm0, paged, p, *, hp):
    b, t, d = x.shape
    n = b * t
    depth = p['w_in'].shape[0]
    wdt = F32 if hp else BF16
    cast = lambda w: w.astype(wdt)
    x2 = x.reshape(n, d)
    tm = min(256, n)
    cos, sin = _rope_tables(pos0 + jnp.arange(t, dtype=jnp.int32))
    t_pad = -(-t // DN_CHUNK) * DN_CHUNK
    tt = min(256, t_pad)
    k_rows, v_rows, ssm_out, conv_out = [], [], [], []
    kv_rows = None
    for l in range(depth):
        lam_init = 0.8 - 0.6 * math.exp(-0.3 * l)
        ws = [cast(w) for w in _split_w_in(p['w_in'][l])]
        lams = (p['lambda_q1'][l], p['lambda_k1'][l], p['lambda_q2'][l], p['lambda_k2'][l])
        if paged is None:
            qkv_pre, z, ba, g_raw, q_s, k_bf, v_bf, k_t, v4 = _in_proj(
                x2, p['attn_norm'][l], ws, p['q_norm'][l], p['k_norm'][l], cos, sin, kv_rows,
                batch=b, tm=tm, layer=l, depth=depth)
            kv_rows = (k_t, v4)
            b_out = _flash(q_s, k_bf, v_bf, lams, p['subln'][l], batch=b, blk=min(512, t), lam_init=lam_init)
        else:
            outs = []
            for grp in (ws[0:3], ws[3:6], ws[6:7]):
                outs += _norm_proj(x2, p['attn_norm'][l], grp, hp=hp, tm=tm)
            qkv_pre, z, ba, q_raw, k_raw, v_raw, g_raw = outs
            q_s, k_rot, _, _ = _qk_prep(q_raw, k_raw, v_raw, p['q_norm'][l], p['k_norm'][l], cos, sin,
                                        hp=hp, tm=tm, q_dtype=F32)
            kpool, vpool, page_table = paged
            b_out = _decode_attn(q_s, k_rot, v_raw, kpool, vpool, page_table, l, lams, p['subln'][l],
                                 lam_init=lam_init, group=math.gcd(16, page_table.shape[1]))
            k_rows.append(k_rot.reshape(b, t, 2 * DA_H, DA_DH))
            v_rows.append(v_raw.reshape(b, t, DA_H, 2 * DA_DH))

        def pad_t(a):
            if t_pad == t:
                return a
            return jnp.pad(a.reshape(b, t, -1), ((0, 0), (0, t_pad - t), (0, 0))).reshape(b * t_pad, -1)

        a_out, s_new = _deltanet(pad_t(qkv_pre), pad_t(z), pad_t(ba), conv0[l], ssm0[l], p['conv_w'][l],
                                 p['a_log'][l], p['dt_bias'][l], p['dn_norm'][l],
                                 batch=b, t_pad=t_pad, t_real=t, tt=tt, hp=hp)
        if t_pad != t:
            a_out = a_out.reshape(b, t_pad, -1)[:, :t].reshape(n, -1)

        moe = l % 2 == 1
        mixed = _mix(x2, a_out, b_out, g_raw, cast(p['w_a_proj'][l]), cast(p['w_b_proj'][l]), cast(p['w_out'][l]),
                     p['ffn_norm'][l], p['w_router'][l // 2] if moe else None, hp=hp, tm=tm, h_dtype=wdt)
        if moe:
            x1, h2, comb = mixed
            x2 = _ffn(x1, h2, comb, cast(p['w_exp_gu'][l // 2]), cast(p['w_exp_down'][l // 2]),
                      hp=hp, tm=min(512, n), tf=p['w_exp_down'].shape[2])
        else:
            x1, h2 = mixed
            x2 = _ffn(x1, h2, None, cast(p['w_ffn_gu'][l // 2][None]), cast(p['w_ffn_down'][l // 2][None]),
                      hp=hp, tm=min(512, n), tf=p['w_ffn_down'].shape[1] // 2)

        ssm_out.append(s_new)
        tail = qkv_pre.reshape(b, t, DN_CH)[:, max(t - (CONV_W - 1), 0):]
        conv_out.append(jnp.concatenate([conv0[l], tail], axis=1)[:, -(CONV_W - 1):])
    if paged is None:
        k_t, v4 = kv_rows
        k_all = jnp.transpose(k_t.reshape(depth, b, 2 * DA_H, DA_DH, t), (0, 1, 4, 2, 3))
        v_all = v4.reshape(depth, b, t, DA_H, 2 * DA_DH)
    else:
        k_all, v_all = jnp.stack(k_rows), jnp.stack(v_rows)
    return x2.reshape(b, t, d), k_all, v_all, jnp.stack(ssm_out), jnp.stack(conv_out)


def kernel(x_prompt, x_sample, cache_k, cache_v, state_ssm, state_conv, page_table, attn_norm, w_in, conv_w,
           a_log, dt_bias, dn_norm, q_norm, k_norm, lambda_q1, lambda_k1, lambda_q2, lambda_k2, subln,
           w_a_proj, w_b_proj, w_out, ffn_norm, w_ffn_gu, w_ffn_down, w_router, w_exp_gu, w_exp_down):
    p = dict(attn_norm=attn_norm, w_in=w_in, conv_w=conv_w, a_log=a_log, dt_bias=dt_bias, dn_norm=dn_norm,
             q_norm=q_norm, k_norm=k_norm, lambda_q1=lambda_q1, lambda_k1=lambda_k1, lambda_q2=lambda_q2,
             lambda_k2=lambda_k2, subln=subln, w_a_proj=w_a_proj, w_b_proj=w_b_proj, w_out=w_out,
             ffn_norm=ffn_norm, w_ffn_gu=w_ffn_gu, w_ffn_down=w_ffn_down, w_router=w_router,
             w_exp_gu=w_exp_gu, w_exp_down=w_exp_down)
    depth = w_in.shape[0]
    bp = x_prompt.shape[0]
    assert x_sample.shape[1] == 1, "the decode path handles one new token per sequence"
    assert cache_k.shape[2] == PAGE
    conv0 = jnp.zeros((depth, bp, CONV_W - 1, DN_CH), F32)
    ssm0 = jnp.zeros((depth, bp, DN_H, DN_DK, DN_DV), F32)
    y_p, k_p, v_p, ssm_p, conv_p = _trunk(x_prompt, 0, conv0, ssm0, None, p, hp=False)

    past_len = page_table.shape[1] * PAGE
    n_pool = cache_k.shape[1]
    kpool = jnp.transpose(cache_k, (0, 1, 3, 4, 2))
    vpool = cache_v.reshape(depth, n_pool, PAGE * DA_H, 2 * DA_DH)
    y_s, k_s, v_s, ssm_s, conv_s = _trunk(x_sample, past_len, state_conv, state_ssm, (kpool, vpool, page_table), p,
                                          hp=True)
    return (y_p, y_s, k_p, v_p, ssm_p, conv_p, k_s, v_s, ssm_s, conv_s)
```

```python
import functools
import math

import jax
import jax.numpy as jnp
from jax import lax
from jax.experimental import pallas as pl
from jax.experimental.pallas import tpu as pltpu

F32 = jnp.float32
BF16 = jnp.bfloat16
EPS = 1e-6
LANES = 128
VMEM_LIMIT = 56 * 1024 * 1024

DN_H, DN_DK, DN_DV = 4, 128, 128
DN_QK = DN_H * DN_DK
DN_CH = 3 * DN_QK
CONV_W = 4
DN_CHUNK = 64
DA_H, DA_DH = 4, 64
DA_W = 2 * DA_H * DA_DH
ROPE_THETA = 10000.0
PAGE = 128

_NN = (((1,), (0,)), ((), ()))
_NT = (((1,), (1,)), ((), ()))
_BNN = (((2,), (1,)), ((0,), (0,)))
_BNT = (((2,), (2,)), ((0,), (0,)))
assert 2 * DA_DH == LANES and DN_DV == LANES


def _cparams(sem):
    return pltpu.CompilerParams(dimension_semantics=sem, vmem_limit_bytes=VMEM_LIMIT)


def _mm(a, b, dims=_NN):
    return lax.dot_general(a, b, dims, preferred_element_type=F32)


def _split(a):
    if a.dtype == BF16:
        return a, None
    hi = a.astype(BF16)
    lo = (a - hi.astype(F32)).astype(BF16)
    return hi, lo


def _dot(a, b, hp, dims=_NN):
    if not hp:
        return _mm(a.astype(BF16), b.astype(BF16), dims)
    ah, al = _split(a)
    bh, bl = _split(b)
    out = _mm(ah, bh, dims)
    if al is not None:
        out = out + _mm(al, bh, dims)
    if bl is not None:
        out = out + _mm(ah, bl, dims)
    return out


def _split3(a):
    hi = a.astype(BF16)
    r = a - hi.astype(F32)
    mid = r.astype(BF16)
    lo = (r - mid.astype(F32)).astype(BF16)
    return hi, mid, lo


def _dot_exact_lhs(a_bf16, b, dims=_NN):
    hi, mid, lo = _split3(b)
    return _mm(a_bf16, hi, dims) + (_mm(a_bf16, mid, dims) + _mm(a_bf16, lo, dims))


def _dot_exact_rhs(a, b_bf16):
    hi, mid, lo = _split3(a)
    return _mm(hi, b_bf16) + (_mm(mid, b_bf16) + _mm(lo, b_bf16))


def _rms(x, g):
    return x * lax.rsqrt(jnp.mean(x * x, axis=-1, keepdims=True) + EPS) * g


def _silu(x):
    return x * jax.nn.sigmoid(x)


def _norm_proj_kernel(x_ref, g_ref, *refs, n_w, hp):
    w_refs, o_refs = refs[:n_w], refs[n_w:]
    h = _rms(x_ref[...], g_ref[...])
    if not hp:
        h = h.astype(BF16)
    for w_ref, o_ref in zip(w_refs, o_refs):
        o_ref[...] = _dot(h, w_ref[...], hp).astype(o_ref.dtype)


def _norm_proj(x, g, ws, *, hp, tm):
    n, d = x.shape
    assert n % tm == 0
    in_specs = [pl.BlockSpec((tm, d), lambda i: (i, 0)), pl.BlockSpec((1, d), lambda i: (0, 0))]
    in_specs += [pl.BlockSpec(w.shape, lambda i: (0, 0)) for w in ws]
    out_specs = [pl.BlockSpec((tm, w.shape[1]), lambda i: (i, 0)) for w in ws]
    out_shape = [jax.ShapeDtypeStruct((n, w.shape[1]), F32) for w in ws]
    return pl.pallas_call(
        functools.partial(_norm_proj_kernel, n_w=len(ws), hp=hp),
        grid=(n // tm,), in_specs=in_specs, out_specs=out_specs, out_shape=out_shape,
        compiler_params=_cparams(("parallel",)), name="norm_proj",
    )(x, g.reshape(1, d), *ws)


def _qk_prep_kernel(q_ref, k_ref, v_ref, qn_ref, kn_ref, cos_ref, sin_ref, bd_ref,
                    qo_ref, ko_ref, kb_ref, vb_ref, *, hp, q_scale):
    tm = q_ref.shape[0]
    lane = lax.broadcasted_iota(jnp.int32, (tm, DA_W), 1)
    first_half = (lane % DA_DH) < (DA_DH // 2)
    bd = bd_ref[...]
    cos, sin = cos_ref[...], sin_ref[...]
    q = _norm_rope(q_ref[...], qn_ref[...], bd, cos, sin, first_half, hp) * q_scale
    k = _norm_rope(k_ref[...], kn_ref[...], bd, cos, sin, first_half, hp)
    qo_ref[...] = q.astype(qo_ref.dtype)
    ko_ref[...] = k
    kb_ref[...] = k.astype(BF16)
    vb_ref[...] = v_ref[...].astype(BF16)


def _rope_tables(pos):
    half = DA_DH // 2
    inv = ROPE_THETA ** (-jnp.arange(half, dtype=F32) / half)
    ang = pos.astype(F32)[:, None] * inv[None, :]
    cos, sin = jnp.cos(ang), jnp.sin(ang)
    cos = jnp.tile(jnp.concatenate([cos, cos], axis=-1), (1, 2 * DA_H))
    sin = jnp.tile(jnp.concatenate([-sin, sin], axis=-1), (1, 2 * DA_H))
    return cos, sin


def _qk_prep(q_raw, k_raw, v_raw, qn, kn, cos, sin, *, hp, tm, q_dtype):
    n = q_raw.shape[0]
    t = cos.shape[0]
    if t % tm:
        cos, sin = jnp.tile(cos, (n // t, 1)), jnp.tile(sin, (n // t, 1))
        t = n
    assert n % tm == 0 and t % tm == 0
    nt = t // tm
    idx = jnp.arange(DA_W) // DA_DH
    bd = (idx[:, None] == idx[None, :]).astype(BF16)
    row = pl.BlockSpec((tm, DA_W), lambda i: (i, 0))
    vec = pl.BlockSpec((1, DA_W), lambda i: (0, 0))
    tab = pl.BlockSpec((tm, DA_W), lambda i: (i % nt, 0))
    return pl.pallas_call(
        functools.partial(_qk_prep_kernel, hp=hp, q_scale=DA_DH ** -0.5),
        grid=(n // tm,),
        in_specs=[row, row, row, vec, vec, tab, tab, pl.BlockSpec((DA_W, DA_W), lambda i: (0, 0))],
        out_specs=[row, row, row, row],
        out_shape=[jax.ShapeDtypeStruct((n, DA_W), q_dtype), jax.ShapeDtypeStruct((n, DA_W), F32),
                   jax.ShapeDtypeStruct((n, DA_W), BF16), jax.ShapeDtypeStruct((n, DA_W), BF16)],
        compiler_params=_cparams(("parallel",)), name="qk_prep",
    )(q_raw, k_raw, v_raw, jnp.tile(qn, 2 * DA_H).reshape(1, DA_W), jnp.tile(kn, 2 * DA_H).reshape(1, DA_W), cos, sin, bd)


def _norm_rope(x, g, bd, cos, sin, first_half, hp):
    sq = x * x
    ss = _dot_exact_rhs(sq, bd) if hp else _mm(sq.astype(BF16), bd)
    y = x * lax.rsqrt(ss * (1.0 / DA_DH) + EPS) * g
    partner = jnp.where(first_half, pltpu.roll(y, DA_W - DA_DH // 2, axis=1), pltpu.roll(y, DA_DH // 2, axis=1))
    return y * cos + partner * sin


def _in_proj_kernel(x_ref, g_ref, wqkv, wz, wba, wq, wk, wv, wg, qn_ref, kn_ref, cos_ref, sin_ref, bd_ref, *refs):
    qkv_o, z_o, ba_o, g_o, q_o, kb_o, vb_o, kt_o, v4_o = refs[-9:]
    tm = x_ref.shape[0]
    h = _rms(x_ref[...], g_ref[...]).astype(BF16)
    qkv_o[...] = _mm(h, wqkv[...])
    z_o[...] = _mm(h, wz[...])
    ba_o[...] = _mm(h, wba[...])
    g_o[...] = jax.nn.sigmoid(_mm(h, wg[...])).astype(BF16)
    lane = lax.broadcasted_iota(jnp.int32, (tm, DA_W), 1)
    first_half = (lane % DA_DH) < (DA_DH // 2)
    bd, cos, sin = bd_ref[...], cos_ref[...], sin_ref[...]
    q = _norm_rope(_mm(h, wq[...]), qn_ref[...], bd, cos, sin, first_half, False) * (DA_DH ** -0.5)
    k = _norm_rope(_mm(h, wk[...]), kn_ref[...], bd, cos, sin, first_half, False)
    v = _mm(h, wv[...])
    q_o[...] = q.astype(BF16)
    kb_o[...] = k.astype(BF16)
    vb_o[...] = v.astype(BF16)
    kt_o[...] = jnp.transpose(k)
    for hh in range(DA_H):
        v4_o[pl.ds(hh, tm, stride=DA_H), :] = v[:, 2 * DA_DH * hh:2 * DA_DH * (hh + 1)]


def _in_proj(x, g, ws, qn, kn, cos, sin, kv_rows, *, batch, tm, layer, depth):
    n, d = x.shape
    t = n // batch
    assert n % tm == 0 and t % tm == 0
    nt = t // tm
    idx = jnp.arange(DA_W) // DA_DH
    bd = (idx[:, None] == idx[None, :]).astype(BF16)
    row = lambda w, dt=F32: (pl.BlockSpec((tm, w), lambda i: (i, 0)), jax.ShapeDtypeStruct((n, w), dt))
    const = lambda shape: pl.BlockSpec(shape, lambda i: (0, 0))
    tab = pl.BlockSpec((tm, DA_W), lambda i: (i % nt, 0))
    outs = [row(DN_CH), row(DN_QK), row(LANES), row(ws[6].shape[1], BF16), row(DA_W, BF16), row(DA_W, BF16), row(DA_W, BF16),
            (pl.BlockSpec((None, None, DA_W, tm), lambda i: (layer, i // nt, 0, i % nt)),
             jax.ShapeDtypeStruct((depth, batch, DA_W, t), F32)),
            (pl.BlockSpec((None, tm * DA_H, 2 * DA_DH), lambda i: (layer, i, 0)),
             jax.ShapeDtypeStruct((depth, n * DA_H, 2 * DA_DH), F32))]
    in_specs = [pl.BlockSpec((tm, d), lambda i: (i, 0)), const((1, d))] + [const(w.shape) for w in ws]
    in_specs += [const((1, DA_W)), const((1, DA_W)), tab, tab, const((DA_W, DA_W))]
    args = [x, g.reshape(1, d), *ws, jnp.tile(qn, 2 * DA_H).reshape(1, DA_W), jnp.tile(kn, 2 * DA_H).reshape(1, DA_W),
            cos, sin, bd]
    aliases = {}
    if kv_rows is not None:
        aliases = {len(args): len(outs) - 2, len(args) + 1: len(outs) - 1}
        in_specs += [pl.BlockSpec(memory_space=pl.ANY)] * 2
        args += list(kv_rows)
    return pl.pallas_call(
        _in_proj_kernel, grid=(n // tm,), in_specs=in_specs,
        out_specs=[o[0] for o in outs], out_shape=[o[1] for o in outs], input_output_aliases=aliases,
        compiler_params=_cparams(("parallel",)), name="in_proj",
    )(*args)


def _lambda(lq1, lk1, lq2, lk2, lam_init):
    return (jnp.exp(jnp.sum(lq1[...] * lk1[...], axis=-1, keepdims=True))
            - jnp.exp(jnp.sum(lq2[...] * lk2[...], axis=-1, keepdims=True)) + lam_init)


def _flash_kernel(qi_ref, kj_ref, q_ref, k_ref, v_ref, lq1, lk1, lq2, lk2, sg_ref, o_ref, qs_scr, m_scr, l_scr,
                  acc_scr, s_scr, p_scr, al_scr, *, blk, lam_init):
    i, j = qi_ref[pl.program_id(1)], kj_ref[pl.program_id(1)]
    hw = 2 * DA_DH
    rb = min(64, blk)

    @pl.when(j == 0)
    def _():
        q = q_ref[...]
        lane = lax.broadcasted_iota(jnp.int32, (blk, hw), 1)
        zero = jnp.zeros((blk, hw), q.dtype)
        for h in range(DA_H):
            qh = q[:, hw * h:hw * (h + 1)]
            qs_scr[h, 0:blk, :] = jnp.where(lane < DA_DH, qh, zero)
            qs_scr[h, blk:2 * blk, :] = jnp.where(lane >= DA_DH, qh, zero)
        m_scr[...] = jnp.full(m_scr.shape, -jnp.inf, F32)
        l_scr[...] = jnp.zeros(l_scr.shape, F32)
        acc_scr[...] = jnp.zeros(acc_scr.shape, F32)

    def step(diag):
        if diag:
            r = lax.broadcasted_iota(jnp.int32, (2 * blk, blk), 0)
            c = lax.broadcasted_iota(jnp.int32, (2 * blk, blk), 1)
            keep = c <= jnp.where(r >= blk, r - blk, r)
        for h in range(DA_H):
            s = _mm(qs_scr[h], k_ref[:, hw * h:hw * (h + 1)], _NT)
            if diag:
                s = jnp.where(keep, s, -jnp.inf)
            s_scr[h] = s
            m_prev = m_scr[h]
            m_new = jnp.maximum(m_prev, jnp.max(s, axis=-1, keepdims=True))
            m_scr[h] = m_new
            al_scr[h] = jnp.exp(m_prev - m_new)
            for t in range(2 * blk // rb):
                rows = slice(t * rb, (t + 1) * rb)
                m_rows = m_scr[h, rows, :]
                p = jnp.exp(s_scr[h, rows, :] - jnp.concatenate([m_rows] * (blk // LANES), axis=1))
                l_scr[h, rows, :] = al_scr[h, rows, :] * l_scr[h, rows, :] + jnp.sum(p, axis=-1, keepdims=True)
                p_scr[h, rows, :] = p.astype(BF16)
            acc_scr[h] = al_scr[h] * acc_scr[h] + _mm(p_scr[h], v_ref[:, hw * h:hw * (h + 1)])

    @pl.when(j < i)
    def _():
        step(False)

    @pl.when(j == i)
    def _():
        step(True)
        lam = _lambda(lq1, lk1, lq2, lk2, lam_init)
        for h in range(DA_H):
            acc = acc_scr[h]
            l = l_scr[h]
            o = acc[0:blk] / l[0:blk] - lam * (acc[blk:] / l[blk:])
            o_ref[:, hw * h:hw * (h + 1)] = _rms(o, sg_ref[...]) * (1.0 - lam_init)


def _flash(q, k, v, lams, subln, *, batch, blk, lam_init):
    n = q.shape[0]
    nq = n // batch // blk
    hw = 2 * DA_DH
    pairs = [(i, j) for i in range(nq) for j in range(i + 1)]
    qi = jnp.array([ij[0] for ij in pairs], jnp.int32)
    kj = jnp.array([ij[1] for ij in pairs], jnp.int32)
    qspec = pl.BlockSpec((blk, DA_W), lambda b, s, qi, kj: (b * nq + qi[s], 0))
    kspec = pl.BlockSpec((blk, DA_W), lambda b, s, qi, kj: (b * nq + kj[s], 0))
    lspec = pl.BlockSpec((1, DA_DH), lambda b, s, qi, kj: (0, 0))
    return pl.pallas_call(
        functools.partial(_flash_kernel, blk=blk, lam_init=lam_init),
        grid_spec=pltpu.PrefetchScalarGridSpec(
            num_scalar_prefetch=2, grid=(batch, len(pairs)),
            in_specs=[qspec, kspec, kspec, lspec, lspec, lspec, lspec,
                      pl.BlockSpec((1, hw), lambda b, s, qi, kj: (0, 0))],
            out_specs=qspec,
            scratch_shapes=[pltpu.VMEM((DA_H, 2 * blk, hw), BF16), pltpu.VMEM((DA_H, 2 * blk, LANES), F32),
                            pltpu.VMEM((DA_H, 2 * blk, LANES), F32), pltpu.VMEM((DA_H, 2 * blk, hw), F32),
                            pltpu.VMEM((DA_H, 2 * blk, blk), F32), pltpu.VMEM((DA_H, 2 * blk, blk), BF16),
                            pltpu.VMEM((DA_H, 2 * blk, LANES), F32)]),
        out_shape=jax.ShapeDtypeStruct((n, DA_W), F32),
        compiler_params=_cparams(("parallel", "arbitrary")), name="flash_diff_attn",
    )(qi, kj, q, k, v, *[x.reshape(1, DA_DH) for x in lams], subln.reshape(1, hw))


def _decode_kernel(pt_ref, qbd_ref, q8_ref, kn_ref, vn_ref, lq1, lk1, lq2, lk2, sg_ref, *refs,
                   group, lam_init):
    k_refs, v_refs = refs[:group], refs[group:2 * group]
    o_ref, m_scr, l_scr, acc_scr = refs[2 * group:]
    s_idx = pl.program_id(1)
    n_maps = 2 * DA_H

    @pl.when(s_idx == 0)
    def _():
        s0 = jnp.sum(q8_ref[...] * kn_ref[...], axis=-1, keepdims=True)
        m_scr[...] = jnp.broadcast_to(s0, (n_maps, LANES))
        l_scr[...] = jnp.ones((n_maps, LANES), F32)
        for h in range(DA_H):
            acc_scr[h] = jnp.broadcast_to(vn_ref[h:h + 1, :], (n_maps, 2 * DA_DH))

    q_hi, q_lo = _split(qbd_ref[...])
    a16 = jnp.concatenate([q_hi, q_lo], axis=0)

    s_list = []
    for g in range(group):
        kt = k_refs[g][...].reshape(n_maps * DA_DH, PAGE)
        k_hi, k_lo = _split(kt)
        r = _mm(a16, k_hi) + _mm(a16, k_lo)
        s_list.append(r[0:n_maps] + r[n_maps:])
    m_prev = m_scr[...][:, :1]
    m_cur = s_list[0].max(axis=-1, keepdims=True)
    for s in s_list[1:]:
        m_cur = jnp.maximum(m_cur, s.max(axis=-1, keepdims=True))
    m_new = jnp.maximum(m_prev, m_cur)
    alpha = jnp.exp(m_prev - m_new)
    l_new = alpha * l_scr[...][:, :1]
    acc = [alpha * acc_scr[h] for h in range(DA_H)]
    for g in range(group):
        p = jnp.exp(s_list[g] - m_new)
        l_new = l_new + jnp.sum(p, axis=-1, keepdims=True)
        p_hi, p_lo = _split(p)
        p16 = jnp.concatenate([p_hi, p_lo], axis=0)
        for h in range(DA_H):
            v_hi, v_lo = _split(v_refs[g][pl.ds(h, PAGE, stride=DA_H), :])
            r = _mm(p16, v_hi) + _mm(p16, v_lo)
            acc[h] = acc[h] + (r[0:n_maps] + r[n_maps:])
    m_scr[...] = jnp.broadcast_to(m_new, (n_maps, LANES))
    l_scr[...] = jnp.broadcast_to(l_new, (n_maps, LANES))
    for h in range(DA_H):
        acc_scr[h] = acc[h]

    @pl.when(s_idx == pl.num_programs(1) - 1)
    def _():
        lam = _lambda(lq1, lk1, lq2, lk2, lam_init)
        for h in range(DA_H):
            o1 = acc[h][2 * h:2 * h + 1] / l_new[2 * h:2 * h + 1]
            o2 = acc[h][2 * h + 1:2 * h + 2] / l_new[2 * h + 1:2 * h + 2]
            o_ref[h:h + 1, :] = _rms(o1 - lam * o2, sg_ref[...]) * (1.0 - lam_init)


def _decode_attn(q, k_new, v_new, kpool, vpool, page_table, layer, lams, subln, *, lam_init, group):
    s, n_pages = page_table.shape
    assert n_pages % group == 0
    hw = 2 * DA_DH
    n_maps = 2 * DA_H
    q8 = q.reshape(s, n_maps, DA_DH)
    qbd = (q8[:, :, None, :] * jnp.eye(n_maps, dtype=F32)[None, :, :, None]).reshape(s, n_maps, DA_W)
    kn8 = k_new.reshape(s, n_maps, DA_DH)
    vn4 = v_new.reshape(s, DA_H, hw)

    small = lambda shape: pl.BlockSpec((None,) + shape, lambda b, t, pt: (b, 0, 0))
    lspec = pl.BlockSpec((1, DA_DH), lambda b, t, pt: (0, 0))
    in_specs = [small((n_maps, DA_W)), small((n_maps, DA_DH)), small((n_maps, DA_DH)), small((DA_H, hw)),
                lspec, lspec, lspec, lspec, pl.BlockSpec((1, hw), lambda b, t, pt: (0, 0))]
    in_specs += [pl.BlockSpec((None, None, n_maps, DA_DH, PAGE),
                              lambda b, t, pt, g=g: (layer, pt[b, t * group + g], 0, 0, 0)) for g in range(group)]
    in_specs += [pl.BlockSpec((None, None, PAGE * DA_H, hw),
                              lambda b, t, pt, g=g: (layer, pt[b, t * group + g], 0, 0)) for g in range(group)]
    out = pl.pallas_call(
        functools.partial(_decode_kernel, group=group, lam_init=lam_init),
        grid_spec=pltpu.PrefetchScalarGridSpec(
            num_scalar_prefetch=1, grid=(s, n_pages // group), in_specs=in_specs,
            out_specs=small((DA_H, hw)),
            scratch_shapes=[pltpu.VMEM((n_maps, LANES), F32), pltpu.VMEM((n_maps, LANES), F32),
                            pltpu.VMEM((DA_H, n_maps, hw), F32)]),
        out_shape=jax.ShapeDtypeStruct((s, DA_H, hw), F32),
        compiler_params=_cparams(("parallel", "arbitrary")), name="decode_diff_attn",
    )(page_table, qbd, q8, kn8, vn4, *[x.reshape(1, DA_DH) for x in lams], subln.reshape(1, hw),
      *([kpool] * group), *([vpool] * group))
    return out.reshape(s, DA_W)


def _tri_inverse(low, eye):
    a = -low
    t = eye + a
    p = a
    steps = int(math.log2(low.shape[-1])) - 1
    for _ in range(steps):
        p = _dot(p, p, True, _BNN)
        t = t + _dot(t, p, True, _BNN)
    return t


def _deltanet_kernel(qkv_ref, z_ref, ba_ref, conv0_ref, s0_ref, cw_ref, alog_ref, dtb_ref, ng_ref,
                     o_ref, s_out_ref, xbuf, s_scr, *, tt, t_real, hp):
    t = pl.program_id(1)
    c = DN_CHUNK

    @pl.when(t == 0)
    def _():
        xbuf[0:8, :] = conv0_ref[...]
        s_scr[...] = s0_ref[...]

    @pl.when(t > 0)
    def _():
        xbuf[0:8, :] = xbuf[tt:tt + 8, :]

    xbuf[8:8 + tt, :] = qkv_ref[...]
    cw = cw_ref[...]
    acc = xbuf[pl.ds(8 - (CONV_W - 1), tt), :] * cw[0:1]
    for jj in range(1, CONV_W):
        acc = acc + xbuf[pl.ds(8 - (CONV_W - 1) + jj, tt), :] * cw[jj:jj + 1]
    act = _silu(acc)

    ba = ba_ref[...]
    beta_all = jax.nn.sigmoid(ba)
    sp_in = ba + dtb_ref[...]
    softplus = jnp.maximum(sp_in, 0.0) + jnp.log(1.0 + jnp.exp(-jnp.abs(sp_in)))
    g_all = -jnp.exp(alog_ref[...]) * softplus
    if t_real is not None:
        ridx = t * tt + lax.broadcasted_iota(jnp.int32, (tt, 1), 0)
        live = ridx < t_real
        act = jnp.where(live, act, 0.0)
        beta_all = jnp.where(live, beta_all, 0.0)
        g_all = jnp.where(live, g_all, 0.0)

    nc = tt // c
    nb = nc * DN_H
    ri = lax.broadcasted_iota(jnp.int32, (nb, c, c), 1)
    ci = lax.broadcasted_iota(jnp.int32, (nb, c, c), 2)
    causal, strict = ri >= ci, ri > ci
    eye = jnp.where(ri == ci, 1.0, 0.0).astype(F32)

    def stack(arr, off, width):
        return jnp.stack([arr[cc * c:(cc + 1) * c, off + width * h:off + width * (h + 1)]
                          for cc in range(nc) for h in range(DN_H)], axis=0)

    rt = lax.broadcasted_iota(jnp.int32, (tt, tt), 0)
    ct = lax.broadcasted_iota(jnp.int32, (tt, tt), 1)
    chunk_tri = jnp.where((rt >= ct) & (rt // c == ct // c), 1.0, 0.0).astype(BF16)
    gc_all = _dot_exact_lhs(chunk_tri, g_all)

    q = stack(act, 0, DN_DK)
    k = stack(act, DN_QK, DN_DK)
    v = stack(act, 2 * DN_QK, DN_DV)
    q = q * lax.rsqrt(jnp.sum(q * q, axis=-1, keepdims=True) + EPS) * (DN_DK ** -0.5)
    k = k * lax.rsqrt(jnp.sum(k * k, axis=-1, keepdims=True) + EPS)
    beta = stack(beta_all, 0, 1)
    gc = stack(gc_all, DN_H, 1)
    gc_t = [jnp.transpose(gc_all[cc * c:(cc + 1) * c, :]) for cc in range(nc)]
    gc_row = jnp.stack([gc_t[cc][DN_H + h:DN_H + h + 1, :] for cc in range(nc) for h in range(DN_H)], axis=0)
    decay = jnp.where(causal, jnp.exp(jnp.where(causal, gc - gc_row, 0.0)), 0.0)
    kb, vb = k * beta, v * beta
    low = jnp.where(strict, _dot(kb, k, hp, _BNT) * decay, 0.0)
    tinv = _tri_inverse(low, eye)
    eg = jnp.exp(gc)
    sol = _dot(tinv, jnp.concatenate([vb, kb * eg], axis=-1), True, _BNN)
    u, w = sol[:, :, :DN_DV], sol[:, :, DN_DV:]
    qk = _dot(q, k, hp, _BNT) * decay
    gc_last = gc[:, c - 1:c, :]
    qd = q * eg
    kd = k * jnp.exp(gc_last - gc)
    gl = jnp.exp(gc_last)
    ng = ng_ref[...]

    s = s_scr[...]
    for cc in range(nc):
        hs = slice(cc * DN_H, (cc + 1) * DN_H)
        v_new = u[hs] - _dot(w[hs], s, hp, _BNN)
        o = _dot(qd[hs], s, hp, _BNN) + _dot(qk[hs], v_new, hp, _BNN)
        kdt = jnp.stack([jnp.transpose(kd[cc * DN_H + h]) for h in range(DN_H)], axis=0)
        s = s * gl[hs] + _dot(kdt, v_new, hp, _BNN)
        for h in range(DN_H):
            zz = z_ref[cc * c:(cc + 1) * c, DN_DV * h:DN_DV * (h + 1)]
            o_ref[cc * c:(cc + 1) * c, DN_DV * h:DN_DV * (h + 1)] = _rms(o[h], ng) * _silu(zz)
    s_scr[...] = s

    @pl.when(t == pl.num_programs(1) - 1)
    def _():
        s_out_ref[...] = s_scr[...]


def _deltanet(qkv, z, ba, conv0, s0, conv_w, a_log, dt_bias, norm_g, *, batch, t_pad, t_real, tt, hp):
    n = qkv.shape[0]
    assert n == batch * t_pad and t_pad % tt == 0 and tt % DN_CHUNK == 0
    nt = t_pad // tt
    conv0p = jnp.pad(conv0, ((0, 0), (8 - (CONV_W - 1), 0), (0, 0)))
    cwp = jnp.pad(conv_w, ((0, 8 - CONV_W), (0, 0)))
    lane_vec = lambda a: jnp.pad(a, (DN_H, LANES - 2 * DN_H)).reshape(1, LANES)
    row = lambda w: pl.BlockSpec((tt, w), lambda b, t: (b * nt + t, 0))
    const = lambda shape: pl.BlockSpec(shape, lambda b, t: (0,) * len(shape))
    a_out, s_out = pl.pallas_call(
        functools.partial(_deltanet_kernel, tt=tt, t_real=None if t_real == t_pad else t_real, hp=hp),
        grid=(batch, nt),
        in_specs=[row(DN_CH), row(DN_QK), row(LANES),
                  pl.BlockSpec((None, 8, DN_CH), lambda b, t: (b, 0, 0)),
                  pl.BlockSpec((None, DN_H, DN_DK, DN_DV), lambda b, t: (b, 0, 0, 0)),
                  const((8, DN_CH)), const((1, LANES)), const((1, LANES)), const((1, DN_DV))],
        out_specs=[row(DN_QK), pl.BlockSpec((None, DN_H, DN_DK, DN_DV), lambda b, t: (b, 0, 0, 0))],
        out_shape=[jax.ShapeDtypeStruct((n, DN_QK), F32), jax.ShapeDtypeStruct(s0.shape, F32)],
        scratch_shapes=[pltpu.VMEM((tt + 8, DN_CH), F32), pltpu.VMEM((DN_H, DN_DK, DN_DV), F32)],
        compiler_params=_cparams(("parallel", "arbitrary")), name="deltanet",
    )(qkv, z, ba, conv0p, s0, cwp, lane_vec(a_log), lane_vec(dt_bias), norm_g.reshape(1, DN_DV))
    return a_out, s_out


def _mix_kernel(x_ref, a_ref, b_ref, ga_ref, gb_ref, wa_ref, wb_ref, wo_ref, fn_ref, *refs, hp, n_experts, gated):
    if n_experts:
        wr_ref, x1_ref, h_ref, comb_ref = refs
    else:
        x1_ref, h_ref = refs
    a = _dot(a_ref[...], wa_ref[...], hp)
    b = _dot(b_ref[...], wb_ref[...], hp)
    gate = (lambda r: r[...].astype(F32)) if gated else (lambda r: jax.nn.sigmoid(r[...]))
    mixed = gate(ga_ref) * a + gate(gb_ref) * b
    x1 = x_ref[...] + _dot(mixed, wo_ref[...], hp)
    x1_ref[...] = x1
    h = _rms(x1, fn_ref[...])
    h_ref[...] = h.astype(h_ref.dtype)
    if n_experts:
        logits = _dot(h, wr_ref[...], True)
        lane = lax.broadcasted_iota(jnp.int32, logits.shape, 1).astype(F32)
        lg = jnp.where(lane < n_experts, logits, -jnp.inf)
        m1 = jnp.max(lg, axis=-1, keepdims=True)
        i1 = jnp.min(jnp.where(lg == m1, lane, float(LANES)), axis=-1, keepdims=True)
        lg2 = jnp.where(lane == i1, -jnp.inf, lg)
        m2 = jnp.max(lg2, axis=-1, keepdims=True)
        i2 = jnp.min(jnp.where(lg2 == m2, lane, float(LANES)), axis=-1, keepdims=True)
        e2 = jnp.exp(m2 - m1)
        den = 1.0 + e2
        comb_ref[...] = jnp.where(lane == i1, 1.0 / den, 0.0) + jnp.where(lane == i2, e2 / den, 0.0)


def _mix(x, a_out, b_out, g_raw, wa, wb, wo, fn, w_router, *, hp, tm, h_dtype, gated):
    n, d = x.shape
    assert n % tm == 0
    n_experts = 0 if w_router is None else w_router.shape[1]
    row = lambda w, c=0: pl.BlockSpec((tm, w), lambda i, c=c: (i, c))
    const = lambda shape: pl.BlockSpec(shape, lambda i: (0, 0))
    in_specs = [row(d), row(a_out.shape[1]), row(b_out.shape[1]), row(d, 0), row(d, 1),
                const(wa.shape), const(wb.shape), const(wo.shape), const((1, d))]
    args = [x, a_out, b_out, g_raw, g_raw, wa, wb, wo, fn.reshape(1, d)]
    out_specs = [row(d), row(d)]
    out_shape = [jax.ShapeDtypeStruct((n, d), F32), jax.ShapeDtypeStruct((n, d), h_dtype)]
    if n_experts:
        in_specs.append(const((d, LANES)))
        args.append(jnp.pad(w_router, ((0, 0), (0, LANES - n_experts))))
        out_specs.append(row(LANES))
        out_shape.append(jax.ShapeDtypeStruct((n, LANES), F32))
    return pl.pallas_call(
        functools.partial(_mix_kernel, hp=hp, n_experts=n_experts, gated=gated),
        grid=(n // tm,), in_specs=in_specs, out_specs=out_specs, out_shape=out_shape,
        compiler_params=_cparams(("parallel",)), name="mix_out",
    )(*args)


def _ffn_kernel(x1_ref, h_ref, *refs, hp, use_comb):
    if use_comb:
        comb_ref, wg_ref, wu_ref, wd_ref, o_ref = refs
    else:
        wg_ref, wu_ref, wd_ref, o_ref = refs
    e, f = pl.program_id(1), pl.program_id(2)

    @pl.when((e == 0) & (f == 0))
    def _():
        o_ref[...] = x1_ref[...]

    h = h_ref[...]
    act = _silu(_dot(h, wg_ref[...], hp)) * _dot(h, wu_ref[...], hp)
    y = _dot(act, wd_ref[...], hp)
    if use_comb:
        comb = comb_ref[...]
        lane = lax.broadcasted_iota(jnp.int32, comb.shape, 1)
        y = y * jnp.sum(jnp.where(lane == e, comb, 0.0), axis=-1, keepdims=True)
    o_ref[...] += y


def _ffn(x1, h, comb, w_gu, w_down, *, hp, tm, tf):
    n, d = x1.shape
    n_e, f_dim = w_down.shape[0], w_down.shape[1]
    assert n % tm == 0 and f_dim % tf == 0
    nf = f_dim // tf
    row = lambda w: pl.BlockSpec((tm, w), lambda i, e, f: (i, 0))
    in_specs = [row(d), row(d)]
    args = [x1, h]
    if comb is not None:
        in_specs.append(row(LANES))
        args.append(comb)
    in_specs += [pl.BlockSpec((None, d, tf), lambda i, e, f: (e, 0, f)),
                 pl.BlockSpec((None, d, tf), lambda i, e, f: (e, 0, f + nf)),
                 pl.BlockSpec((None, tf, d), lambda i, e, f: (e, f, 0))]
    args += [w_gu, w_gu, w_down]
    return pl.pallas_call(
        functools.partial(_ffn_kernel, hp=hp, use_comb=comb is not None),
        grid=(n // tm, n_e, nf), in_specs=in_specs, out_specs=row(d),
        out_shape=jax.ShapeDtypeStruct((n, d), F32),
        compiler_params=_cparams(("parallel", "arbitrary", "arbitrary")), name="ffn",
    )(*args)


def _split_w_in(w):
    o = [0]
    for s in (DN_CH, DN_QK, DN_H, DN_H, DA_W, DA_W, DA_W):
        o.append(o[-1] + s)
    ba = jnp.pad(w[:, o[2]:o[4]], ((0, 0), (0, LANES - 2 * DN_H)))
    return [w[:, o[0]:o[1]], w[:, o[1]:o[2]], ba, w[:, o[4]:o[5]], w[:, o[5]:o[6]], w[:, o[6]:o[7]], w[:, o[7]:]]


def _trunk(x, pos0, conv0, ssm0, paged, p, *, hp):
    b, t, d = x.shape
    n = b * t
    depth = p['w_in'].shape[0]
    wdt = F32 if hp else BF16
    cast = lambda w: w.astype(wdt)
    x2 = x.reshape(n, d)
    tm = min(256, n)
    cos, sin = _rope_tables(pos0 + jnp.arange(t, dtype=jnp.int32))
    t_pad = -(-t // DN_CHUNK) * DN_CHUNK
    tt = min(256, t_pad)
    k_rows, v_rows, ssm_out, conv_out = [], [], [], []
    kv_rows = None
    for l in range(depth):
        lam_init = 0.8 - 0.6 * math.exp(-0.3 * l)
        ws = [cast(w) for w in _split_w_in(p['w_in'][l])]
        lams = (p['lambda_q1'][l], p['lambda_k1'][l], p['lambda_q2'][l], p['lambda_k2'][l])
        if paged is None:
            qkv_pre, z, ba, g_raw, q_s, k_bf, v_bf, k_t, v4 = _in_proj(
                x2, p['attn_norm'][l], ws, p['q_norm'][l], p['k_norm'][l], cos, sin, kv_rows,
                batch=b, tm=tm, layer=l, depth=depth)
            kv_rows = (k_t, v4)
            b_out = _flash(q_s, k_bf, v_bf, lams, p['subln'][l], batch=b, blk=min(512, t), lam_init=lam_init)
        else:
            outs = []
            for grp in (ws[0:3], ws[3:6], ws[6:7]):
                outs += _norm_proj(x2, p['attn_norm'][l], grp, hp=hp, tm=tm)
            qkv_pre, z, ba, q_raw, k_raw, v_raw, g_raw = outs
            q_s, k_rot, _, _ = _qk_prep(q_raw, k_raw, v_raw, p['q_norm'][l], p['k_norm'][l], cos, sin,
                                        hp=hp, tm=tm, q_dtype=F32)
            kpool, vpool, page_table = paged
            b_out = _decode_attn(q_s, k_rot, v_raw, kpool, vpool, page_table, l, lams, p['subln'][l],
                                 lam_init=lam_init, group=math.gcd(32, page_table.shape[1]))
            k_rows.append(k_rot.reshape(b, t, 2 * DA_H, DA_DH))
            v_rows.append(v_raw.reshape(b, t, DA_H, 2 * DA_DH))

        def pad_t(a):
            if t_pad == t:
                return a
            return jnp.pad(a.reshape(b, t, -1), ((0, 0), (0, t_pad - t), (0, 0))).reshape(b * t_pad, -1)

        a_out, s_new = _deltanet(pad_t(qkv_pre), pad_t(z), pad_t(ba), conv0[l], ssm0[l], p['conv_w'][l],
                                 p['a_log'][l], p['dt_bias'][l], p['dn_norm'][l],
                                 batch=b, t_pad=t_pad, t_real=t, tt=tt, hp=hp)
        if t_pad != t:
            a_out = a_out.reshape(b, t_pad, -1)[:, :t].reshape(n, -1)

        moe = l % 2 == 1
        mixed = _mix(x2, a_out, b_out, g_raw, cast(p['w_a_proj'][l]), cast(p['w_b_proj'][l]), cast(p['w_out'][l]),
                     p['ffn_norm'][l], p['w_router'][l // 2] if moe else None, hp=hp, tm=min(512, n), h_dtype=wdt,
                     gated=paged is None)
        if moe:
            x1, h2, comb = mixed
            x2 = _ffn(x1, h2, comb, cast(p['w_exp_gu'][l // 2]), cast(p['w_exp_down'][l // 2]),
                      hp=hp, tm=min(1024, n), tf=p['w_exp_down'].shape[2])
        else:
            x1, h2 = mixed
            x2 = _ffn(x1, h2, None, cast(p['w_ffn_gu'][l // 2][None]), cast(p['w_ffn_down'][l // 2][None]),
                      hp=hp, tm=min(1024, n), tf=p['w_ffn_down'].shape[1] // 2)

        ssm_out.append(s_new)
        tail = qkv_pre.reshape(b, t, DN_CH)[:, max(t - (CONV_W - 1), 0):]
        conv_out.append(jnp.concatenate([conv0[l], tail], axis=1)[:, -(CONV_W - 1):])
    if paged is None:
        k_t, v4 = kv_rows
        k_all = jnp.transpose(k_t.reshape(depth, b, 2 * DA_H, DA_DH, t), (0, 1, 4, 2, 3))
        v_all = v4.reshape(depth, b, t, DA_H, 2 * DA_DH)
    else:
        k_all, v_all = jnp.stack(k_rows), jnp.stack(v_rows)
    return x2.reshape(b, t, d), k_all, v_all, jnp.stack(ssm_out), jnp.stack(conv_out)


def kernel(x_prompt, x_sample, cache_k, cache_v, state_ssm, state_conv, page_table, attn_norm, w_in, conv_w,
           a_log, dt_bias, dn_norm, q_norm, k_norm, lambda_q1, lambda_k1, lambda_q2, lambda_k2, subln,
           w_a_proj, w_b_proj, w_out, ffn_norm, w_ffn_gu, w_ffn_down, w_router, w_exp_gu, w_exp_down):
    p = dict(attn_norm=attn_norm, w_in=w_in, conv_w=conv_w, a_log=a_log, dt_bias=dt_bias, dn_norm=dn_norm,
             q_norm=q_norm, k_norm=k_norm, lambda_q1=lambda_q1, lambda_k1=lambda_k1, lambda_q2=lambda_q2,
             lambda_k2=lambda_k2, subln=subln, w_a_proj=w_a_proj, w_b_proj=w_b_proj, w_out=w_out,
             ffn_norm=ffn_norm, w_ffn_gu=w_ffn_gu, w_ffn_down=w_ffn_down, w_router=w_router,
             w_exp_gu=w_exp_gu, w_exp_down=w_exp_down)
    depth = w_in.shape[0]
    bp = x_prompt.shape[0]
    assert x_sample.shape[1] == 1, "the decode path handles one new token per sequence"
    assert cache_k.shape[2] == PAGE
    conv0 = jnp.zeros((depth, bp, CONV_W - 1, DN_CH), F32)
    ssm0 = jnp.zeros((depth, bp, DN_H, DN_DK, DN_DV), F32)
    y_p, k_p, v_p, ssm_p, conv_p = _trunk(x_prompt, 0, conv0, ssm0, None, p, hp=False)

    past_len = page_table.shape[1] * PAGE
    n_pool = cache_k.shape[1]
    kpool = jnp.transpose(cache_k, (0, 1, 3, 4, 2))
    vpool = cache_v.reshape(depth, n_pool, PAGE * DA_H, 2 * DA_DH)
    y_s, k_s, v_s, ssm_s, conv_s = _trunk(x_sample, past_len, state_conv, state_ssm, (kpool, vpool, page_table), p,
                                          hp=True)
    return (y_p, y_s, k_p, v_p, ssm_p, conv_p, k_s, v_s, ssm_s, conv_s)
```

```python
import functools
import math

import jax
import jax.numpy as jnp
from jax import lax
from jax.experimental import pallas as pl
from jax.experimental.pallas import tpu as pltpu

F32 = jnp.float32
BF16 = jnp.bfloat16
EPS = 1e-6
LANES = 128
VMEM_LIMIT = 56 * 1024 * 1024

DN_H, DN_DK, DN_DV = 4, 128, 128
DN_QK = DN_H * DN_DK
DN_CH = 3 * DN_QK
CONV_W = 4
DN_CHUNK = 64
DA_H, DA_DH = 4, 64
DA_W = 2 * DA_H * DA_DH
ROPE_THETA = 10000.0
PAGE = 128

_NN = (((1,), (0,)), ((), ()))
_NT = (((1,), (1,)), ((), ()))
_BNN = (((2,), (1,)), ((0,), (0,)))
_BNT = (((2,), (2,)), ((0,), (0,)))
assert 2 * DA_DH == LANES and DN_DV == LANES


def _cparams(sem):
    return pltpu.CompilerParams(dimension_semantics=sem, vmem_limit_bytes=VMEM_LIMIT)


def _mm(a, b, dims=_NN):
    return lax.dot_general(a, b, dims, preferred_element_type=F32)


def _split(a):
    if a.dtype == BF16:
        return a, None
    hi = a.astype(BF16)
    lo = (a - hi.astype(F32)).astype(BF16)
    return hi, lo


def _dot(a, b, hp, dims=_NN):
    if not hp:
        return _mm(a.astype(BF16), b.astype(BF16), dims)
    ah, al = _split(a)
    bh, bl = _split(b)
    out = _mm(ah, bh, dims)
    if al is not None:
        out = out + _mm(al, bh, dims)
    if bl is not None:
        out = out + _mm(ah, bl, dims)
    return out


def _split3(a):
    hi = a.astype(BF16)
    r = a - hi.astype(F32)
    mid = r.astype(BF16)
    lo = (r - mid.astype(F32)).astype(BF16)
    return hi, mid, lo


def _dot_exact_lhs(a_bf16, b, dims=_NN):
    hi, mid, lo = _split3(b)
    return _mm(a_bf16, hi, dims) + (_mm(a_bf16, mid, dims) + _mm(a_bf16, lo, dims))


def _dot_exact_rhs(a, b_bf16):
    hi, mid, lo = _split3(a)
    return _mm(hi, b_bf16) + (_mm(mid, b_bf16) + _mm(lo, b_bf16))


def _rms(x, g):
    return x * lax.rsqrt(jnp.mean(x * x, axis=-1, keepdims=True) + EPS) * g


def _silu(x):
    return x * jax.nn.sigmoid(x)


def _norm_proj_kernel(x_ref, g_ref, *refs, n_w, hp):
    w_refs, o_refs = refs[:n_w], refs[n_w:]
    h = _rms(x_ref[...], g_ref[...])
    if not hp:
        h = h.astype(BF16)
    for w_ref, o_ref in zip(w_refs, o_refs):
        o_ref[...] = _dot(h, w_ref[...], hp).astype(o_ref.dtype)


def _norm_proj(x, g, ws, *, hp, tm):
    n, d = x.shape
    assert n % tm == 0
    in_specs = [pl.BlockSpec((tm, d), lambda i: (i, 0)), pl.BlockSpec((1, d), lambda i: (0, 0))]
    in_specs += [pl.BlockSpec(w.shape, lambda i: (0, 0)) for w in ws]
    out_specs = [pl.BlockSpec((tm, w.shape[1]), lambda i: (i, 0)) for w in ws]
    out_shape = [jax.ShapeDtypeStruct((n, w.shape[1]), F32) for w in ws]
    return pl.pallas_call(
        functools.partial(_norm_proj_kernel, n_w=len(ws), hp=hp),
        grid=(n // tm,), in_specs=in_specs, out_specs=out_specs, out_shape=out_shape,
        compiler_params=_cparams(("parallel",)), name="norm_proj",
    )(x, g.reshape(1, d), *ws)


def _qk_prep_kernel(q_ref, k_ref, v_ref, qn_ref, kn_ref, cos_ref, sin_ref, bd_ref,
                    qo_ref, ko_ref, kb_ref, vb_ref, *, hp, q_scale):
    tm = q_ref.shape[0]
    lane = lax.broadcasted_iota(jnp.int32, (tm, DA_W), 1)
    first_half = (lane % DA_DH) < (DA_DH // 2)
    bd = bd_ref[...]
    cos, sin = cos_ref[...], sin_ref[...]
    q = _norm_rope(q_ref[...], qn_ref[...], bd, cos, sin, first_half, hp) * q_scale
    k = _norm_rope(k_ref[...], kn_ref[...], bd, cos, sin, first_half, hp)
    qo_ref[...] = q.astype(qo_ref.dtype)
    ko_ref[...] = k
    kb_ref[...] = k.astype(BF16)
    vb_ref[...] = v_ref[...].astype(BF16)


def _rope_tables(pos):
    half = DA_DH // 2
    inv = ROPE_THETA ** (-jnp.arange(half, dtype=F32) / half)
    ang = pos.astype(F32)[:, None] * inv[None, :]
    cos, sin = jnp.cos(ang), jnp.sin(ang)
    cos = jnp.tile(jnp.concatenate([cos, cos], axis=-1), (1, 2 * DA_H))
    sin = jnp.tile(jnp.concatenate([-sin, sin], axis=-1), (1, 2 * DA_H))
    return cos, sin


def _qk_prep(q_raw, k_raw, v_raw, qn, kn, cos, sin, *, hp, tm, q_dtype):
    n = q_raw.shape[0]
    t = cos.shape[0]
    if t % tm:
        cos, sin = jnp.tile(cos, (n // t, 1)), jnp.tile(sin, (n // t, 1))
        t = n
    assert n % tm == 0 and t % tm == 0
    nt = t // tm
    idx = jnp.arange(DA_W) // DA_DH
    bd = (idx[:, None] == idx[None, :]).astype(BF16)
    row = pl.BlockSpec((tm, DA_W), lambda i: (i, 0))
    vec = pl.BlockSpec((1, DA_W), lambda i: (0, 0))
    tab = pl.BlockSpec((tm, DA_W), lambda i: (i % nt, 0))
    return pl.pallas_call(
        functools.partial(_qk_prep_kernel, hp=hp, q_scale=DA_DH ** -0.5),
        grid=(n // tm,),
        in_specs=[row, row, row, vec, vec, tab, tab, pl.BlockSpec((DA_W, DA_W), lambda i: (0, 0))],
        out_specs=[row, row, row, row],
        out_shape=[jax.ShapeDtypeStruct((n, DA_W), q_dtype), jax.ShapeDtypeStruct((n, DA_W), F32),
                   jax.ShapeDtypeStruct((n, DA_W), BF16), jax.ShapeDtypeStruct((n, DA_W), BF16)],
        compiler_params=_cparams(("parallel",)), name="qk_prep",
    )(q_raw, k_raw, v_raw, jnp.tile(qn, 2 * DA_H).reshape(1, DA_W), jnp.tile(kn, 2 * DA_H).reshape(1, DA_W), cos, sin, bd)


def _norm_rope(x, g, bd, cos, sin, first_half, hp):
    sq = x * x
    ss = _dot_exact_rhs(sq, bd) if hp else _mm(sq.astype(BF16), bd)
    y = x * lax.rsqrt(ss * (1.0 / DA_DH) + EPS) * g
    partner = jnp.where(first_half, pltpu.roll(y, DA_W - DA_DH // 2, axis=1), pltpu.roll(y, DA_DH // 2, axis=1))
    return y * cos + partner * sin


def _in_proj_kernel(x_ref, g_ref, wqkv, wz, wba, wq, wk, wv, wg, qn_ref, kn_ref, cos_ref, sin_ref, bd_ref, *refs):
    qkv_o, z_o, ba_o, g_o, q_o, kb_o, vb_o, kt_o, v4_o = refs[-9:]
    tm = x_ref.shape[0]
    h = _rms(x_ref[...], g_ref[...]).astype(BF16)
    qkv_o[...] = _mm(h, wqkv[...])
    z_o[...] = _mm(h, wz[...])
    ba_o[...] = _mm(h, wba[...])
    g_o[...] = jax.nn.sigmoid(_mm(h, wg[...])).astype(BF16)
    lane = lax.broadcasted_iota(jnp.int32, (tm, DA_W), 1)
    first_half = (lane % DA_DH) < (DA_DH // 2)
    bd, cos, sin = bd_ref[...], cos_ref[...], sin_ref[...]
    q = _norm_rope(_mm(h, wq[...]), qn_ref[...], bd, cos, sin, first_half, False) * (DA_DH ** -0.5)
    k = _norm_rope(_mm(h, wk[...]), kn_ref[...], bd, cos, sin, first_half, False)
    v = _mm(h, wv[...])
    q_o[...] = q.astype(BF16)
    kb_o[...] = k.astype(BF16)
    vb_o[...] = v.astype(BF16)
    kt_o[...] = jnp.transpose(k)
    for hh in range(DA_H):
        v4_o[pl.ds(hh, tm, stride=DA_H), :] = v[:, 2 * DA_DH * hh:2 * DA_DH * (hh + 1)]


def _in_proj(x, g, ws, qn, kn, cos, sin, kv_rows, *, batch, tm, layer, depth):
    n, d = x.shape
    t = n // batch
    assert n % tm == 0 and t % tm == 0
    nt = t // tm
    idx = jnp.arange(DA_W) // DA_DH
    bd = (idx[:, None] == idx[None, :]).astype(BF16)
    row = lambda w, dt=F32: (pl.BlockSpec((tm, w), lambda i: (i, 0)), jax.ShapeDtypeStruct((n, w), dt))
    const = lambda shape: pl.BlockSpec(shape, lambda i: (0, 0))
    tab = pl.BlockSpec((tm, DA_W), lambda i: (i % nt, 0))
    outs = [row(DN_CH), row(DN_QK), row(LANES), row(ws[6].shape[1], BF16), row(DA_W, BF16), row(DA_W, BF16), row(DA_W, BF16),
            (pl.BlockSpec((None, None, DA_W, tm), lambda i: (layer, i // nt, 0, i % nt)),
             jax.ShapeDtypeStruct((depth, batch, DA_W, t), F32)),
            (pl.BlockSpec((None, tm * DA_H, 2 * DA_DH), lambda i: (layer, i, 0)),
             jax.ShapeDtypeStruct((depth, n * DA_H, 2 * DA_DH), F32))]
    in_specs = [pl.BlockSpec((tm, d), lambda i: (i, 0)), const((1, d))] + [const(w.shape) for w in ws]
    in_specs += [const((1, DA_W)), const((1, DA_W)), tab, tab, const((DA_W, DA_W))]
    args = [x, g.reshape(1, d), *ws, jnp.tile(qn, 2 * DA_H).reshape(1, DA_W), jnp.tile(kn, 2 * DA_H).reshape(1, DA_W),
            cos, sin, bd]
    aliases = {}
    if kv_rows is not None:
        aliases = {len(args): len(outs) - 2, len(args) + 1: len(outs) - 1}
        in_specs += [pl.BlockSpec(memory_space=pl.ANY)] * 2
        args += list(kv_rows)
    return pl.pallas_call(
        _in_proj_kernel, grid=(n // tm,), in_specs=in_specs,
        out_specs=[o[0] for o in outs], out_shape=[o[1] for o in outs], input_output_aliases=aliases,
        compiler_params=_cparams(("parallel",)), name="in_proj",
    )(*args)


def _lambda(lq1, lk1, lq2, lk2, lam_init):
    return (jnp.exp(jnp.sum(lq1[...] * lk1[...], axis=-1, keepdims=True))
            - jnp.exp(jnp.sum(lq2[...] * lk2[...], axis=-1, keepdims=True)) + lam_init)


def _flash_kernel(qi_ref, kj_ref, q_ref, k_ref, v_ref, lq1, lk1, lq2, lk2, sg_ref, o_ref, qs_scr, m_scr, l_scr,
                  acc_scr, s_scr, p_scr, al_scr, *, blk, lam_init):
    i, j = qi_ref[pl.program_id(1)], kj_ref[pl.program_id(1)]
    hw = 2 * DA_DH
    rb = min(64, blk)

    @pl.when(j == 0)
    def _():
        q = q_ref[...]
        lane = lax.broadcasted_iota(jnp.int32, (blk, hw), 1)
        zero = jnp.zeros((blk, hw), q.dtype)
        for h in range(DA_H):
            qh = q[:, hw * h:hw * (h + 1)]
            qs_scr[h, 0:blk, :] = jnp.where(lane < DA_DH, qh, zero)
            qs_scr[h, blk:2 * blk, :] = jnp.where(lane >= DA_DH, qh, zero)
        m_scr[...] = jnp.full(m_scr.shape, -jnp.inf, F32)
        l_scr[...] = jnp.zeros(l_scr.shape, F32)
        acc_scr[...] = jnp.zeros(acc_scr.shape, F32)

    def step(diag):
        if diag:
            r = lax.broadcasted_iota(jnp.int32, (2 * blk, blk), 0)
            c = lax.broadcasted_iota(jnp.int32, (2 * blk, blk), 1)
            keep = c <= jnp.where(r >= blk, r - blk, r)
        for h in range(DA_H):
            s = _mm(qs_scr[h], k_ref[:, hw * h:hw * (h + 1)], _NT)
            if diag:
                s = jnp.where(keep, s, -jnp.inf)
            s_scr[h] = s
            m_prev = m_scr[h]
            m_new = jnp.maximum(m_prev, jnp.max(s, axis=-1, keepdims=True))
            m_scr[h] = m_new
            al_scr[h] = jnp.exp(m_prev - m_new)
            for t in range(2 * blk // rb):
                rows = slice(t * rb, (t + 1) * rb)
                m_rows = m_scr[h, rows, :]
                p = jnp.exp(s_scr[h, rows, :] - jnp.concatenate([m_rows] * (blk // LANES), axis=1))
                l_scr[h, rows, :] = al_scr[h, rows, :] * l_scr[h, rows, :] + jnp.sum(p, axis=-1, keepdims=True)
                p_scr[h, rows, :] = p.astype(BF16)
            acc_scr[h] = al_scr[h] * acc_scr[h] + _mm(p_scr[h], v_ref[:, hw * h:hw * (h + 1)])

    @pl.when(j < i)
    def _():
        step(False)

    @pl.when(j == i)
    def _():
        step(True)
        lam = _lambda(lq1, lk1, lq2, lk2, lam_init)
        for h in range(DA_H):
            acc = acc_scr[h]
            l = l_scr[h]
            o = acc[0:blk] / l[0:blk] - lam * (acc[blk:] / l[blk:])
            o_ref[:, hw * h:hw * (h + 1)] = _rms(o, sg_ref[...]) * (1.0 - lam_init)


def _flash(q, k, v, lams, subln, *, batch, blk, lam_init):
    n = q.shape[0]
    nq = n // batch // blk
    hw = 2 * DA_DH
    pairs = [(i, j) for i in range(nq) for j in range(i + 1)]
    qi = jnp.array([ij[0] for ij in pairs], jnp.int32)
    kj = jnp.array([ij[1] for ij in pairs], jnp.int32)
    qspec = pl.BlockSpec((blk, DA_W), lambda b, s, qi, kj: (b * nq + qi[s], 0))
    kspec = pl.BlockSpec((blk, DA_W), lambda b, s, qi, kj: (b * nq + kj[s], 0))
    lspec = pl.BlockSpec((1, DA_DH), lambda b, s, qi, kj: (0, 0))
    return pl.pallas_call(
        functools.partial(_flash_kernel, blk=blk, lam_init=lam_init),
        grid_spec=pltpu.PrefetchScalarGridSpec(
            num_scalar_prefetch=2, grid=(batch, len(pairs)),
            in_specs=[qspec, kspec, kspec, lspec, lspec, lspec, lspec,
                      pl.BlockSpec((1, hw), lambda b, s, qi, kj: (0, 0))],
            out_specs=qspec,
            scratch_shapes=[pltpu.VMEM((DA_H, 2 * blk, hw), BF16), pltpu.VMEM((DA_H, 2 * blk, LANES), F32),
                            pltpu.VMEM((DA_H, 2 * blk, LANES), F32), pltpu.VMEM((DA_H, 2 * blk, hw), F32),
                            pltpu.VMEM((DA_H, 2 * blk, blk), F32), pltpu.VMEM((DA_H, 2 * blk, blk), BF16),
                            pltpu.VMEM((DA_H, 2 * blk, LANES), F32)]),
        out_shape=jax.ShapeDtypeStruct((n, DA_W), F32),
        compiler_params=_cparams(("parallel", "arbitrary")), name="flash_diff_attn",
    )(qi, kj, q, k, v, *[x.reshape(1, DA_DH) for x in lams], subln.reshape(1, hw))


def _decode_kernel(pt_ref, qbd_ref, q8_ref, kn_ref, vn_ref, lq1, lk1, lq2, lk2, sg_ref, *refs,
                   group, lam_init):
    k_refs, v_refs = refs[:group], refs[group:2 * group]
    o_ref, m_scr, l_scr, acc_scr = refs[2 * group:]
    s_idx = pl.program_id(1)
    n_maps = 2 * DA_H

    @pl.when(s_idx == 0)
    def _():
        s0 = jnp.sum(q8_ref[...] * kn_ref[...], axis=-1, keepdims=True)
        m_scr[...] = jnp.broadcast_to(s0, (n_maps, LANES))
        l_scr[...] = jnp.ones((n_maps, LANES), F32)
        for h in range(DA_H):
            acc_scr[h] = jnp.broadcast_to(vn_ref[h:h + 1, :], (n_maps, 2 * DA_DH))

    q_hi, q_lo = _split(qbd_ref[...])
    a16 = jnp.concatenate([q_hi, q_lo], axis=0)

    s_list = []
    for g in range(group):
        kt = k_refs[g][...].reshape(n_maps * DA_DH, PAGE)
        k_hi, k_lo = _split(kt)
        r = _mm(a16, k_hi) + _mm(a16, k_lo)
        s_list.append(r[0:n_maps] + r[n_maps:])
    m_prev = m_scr[...][:, :1]
    m_cur = s_list[0].max(axis=-1, keepdims=True)
    for s in s_list[1:]:
        m_cur = jnp.maximum(m_cur, s.max(axis=-1, keepdims=True))
    m_new = jnp.maximum(m_prev, m_cur)
    alpha = jnp.exp(m_prev - m_new)
    l_new = alpha * l_scr[...][:, :1]
    acc = [alpha * acc_scr[h] for h in range(DA_H)]
    for g in range(group):
        p = jnp.exp(s_list[g] - m_new)
        l_new = l_new + jnp.sum(p, axis=-1, keepdims=True)
        p_hi, p_lo = _split(p)
        p16 = jnp.concatenate([p_hi, p_lo], axis=0)
        for h in range(DA_H):
            v_hi, v_lo = _split(v_refs[g][pl.ds(h, PAGE, stride=DA_H), :])
            r = _mm(p16, v_hi) + _mm(p16, v_lo)
            acc[h] = acc[h] + (r[0:n_maps] + r[n_maps:])
    m_scr[...] = jnp.broadcast_to(m_new, (n_maps, LANES))
    l_scr[...] = jnp.broadcast_to(l_new, (n_maps, LANES))
    for h in range(DA_H):
        acc_scr[h] = acc[h]

    @pl.when(s_idx == pl.num_programs(1) - 1)
    def _():
        lam = _lambda(lq1, lk1, lq2, lk2, lam_init)
        for h in range(DA_H):
            o1 = acc[h][2 * h:2 * h + 1] / l_new[2 * h:2 * h + 1]
            o2 = acc[h][2 * h + 1:2 * h + 2] / l_new[2 * h + 1:2 * h + 2]
            o_ref[h:h + 1, :] = _rms(o1 - lam * o2, sg_ref[...]) * (1.0 - lam_init)


def _decode_attn(q, k_new, v_new, kpool, vpool, page_table, layer, lams, subln, *, lam_init, group):
    s, n_pages = page_table.shape
    assert n_pages % group == 0
    hw = 2 * DA_DH
    n_maps = 2 * DA_H
    q8 = q.reshape(s, n_maps, DA_DH)
    qbd = (q8[:, :, None, :] * jnp.eye(n_maps, dtype=F32)[None, :, :, None]).reshape(s, n_maps, DA_W)
    kn8 = k_new.reshape(s, n_maps, DA_DH)
    vn4 = v_new.reshape(s, DA_H, hw)

    small = lambda shape: pl.BlockSpec((None,) + shape, lambda b, t, pt: (b, 0, 0))
    lspec = pl.BlockSpec((1, DA_DH), lambda b, t, pt: (0, 0))
    in_specs = [small((n_maps, DA_W)), small((n_maps, DA_DH)), small((n_maps, DA_DH)), small((DA_H, hw)),
                lspec, lspec, lspec, lspec, pl.BlockSpec((1, hw), lambda b, t, pt: (0, 0))]
    in_specs += [pl.BlockSpec((None, None, n_maps, DA_DH, PAGE),
                              lambda b, t, pt, g=g: (layer, pt[b, t * group + g], 0, 0, 0)) for g in range(group)]
    in_specs += [pl.BlockSpec((None, None, PAGE * DA_H, hw),
                              lambda b, t, pt, g=g: (layer, pt[b, t * group + g], 0, 0)) for g in range(group)]
    out = pl.pallas_call(
        functools.partial(_decode_kernel, group=group, lam_init=lam_init),
        grid_spec=pltpu.PrefetchScalarGridSpec(
            num_scalar_prefetch=1, grid=(s, n_pages // group), in_specs=in_specs,
            out_specs=small((DA_H, hw)),
            scratch_shapes=[pltpu.VMEM((n_maps, LANES), F32), pltpu.VMEM((n_maps, LANES), F32),
                            pltpu.VMEM((DA_H, n_maps, hw), F32)]),
        out_shape=jax.ShapeDtypeStruct((s, DA_H, hw), F32),
        compiler_params=_cparams(("parallel", "arbitrary")), name="decode_diff_attn",
    )(page_table, qbd, q8, kn8, vn4, *[x.reshape(1, DA_DH) for x in lams], subln.reshape(1, hw),
      *([kpool] * group), *([vpool] * group))
    return out.reshape(s, DA_W)


def _tri_inverse(low, eye):
    a = -low
    t = eye + a
    ph, pl_ = _split(a)
    steps = int(math.log2(low.shape[-1])) - 1
    for _ in range(steps):
        p = _mm(ph, ph, _BNN) + (_mm(pl_, ph, _BNN) + _mm(ph, pl_, _BNN))
        ph, pl_ = _split(p)
        th, tl = _split(t)
        t = t + (_mm(th, ph, _BNN) + (_mm(tl, ph, _BNN) + _mm(th, pl_, _BNN)))
    return t


def _deltanet_kernel(qkv_ref, z_ref, ba_ref, conv0_ref, s0_ref, cw_ref, alog_ref, dtb_ref, ng_ref,
                     o_ref, s_out_ref, xbuf, s_scr, *, tt, t_real, hp):
    t = pl.program_id(1)
    c = min(DN_CHUNK, tt)

    @pl.when(t == 0)
    def _():
        xbuf[0:8, :] = conv0_ref[...]
        s_scr[...] = s0_ref[...]

    @pl.when(t > 0)
    def _():
        xbuf[0:8, :] = xbuf[tt:tt + 8, :]

    xbuf[8:8 + tt, :] = qkv_ref[...]
    cw = cw_ref[...]
    acc = xbuf[pl.ds(8 - (CONV_W - 1), tt), :] * cw[0:1]
    for jj in range(1, CONV_W):
        acc = acc + xbuf[pl.ds(8 - (CONV_W - 1) + jj, tt), :] * cw[jj:jj + 1]
    act = _silu(acc)

    ba = ba_ref[...]
    beta_all = jax.nn.sigmoid(ba)
    sp_in = ba + dtb_ref[...]
    softplus = jnp.maximum(sp_in, 0.0) + jnp.log(1.0 + jnp.exp(-jnp.abs(sp_in)))
    g_all = -jnp.exp(alog_ref[...]) * softplus
    if t_real is not None:
        ridx = t * tt + lax.broadcasted_iota(jnp.int32, (tt, 1), 0)
        live = ridx < t_real
        act = jnp.where(live, act, 0.0)
        beta_all = jnp.where(live, beta_all, 0.0)
        g_all = jnp.where(live, g_all, 0.0)

    nc = tt // c
    nb = nc * DN_H
    ri = lax.broadcasted_iota(jnp.int32, (nb, c, c), 1)
    ci = lax.broadcasted_iota(jnp.int32, (nb, c, c), 2)
    causal, strict = ri >= ci, ri > ci
    eye = jnp.where(ri == ci, 1.0, 0.0).astype(F32)

    def stack(arr, off, width):
        return jnp.stack([arr[cc * c:(cc + 1) * c, off + width * h:off + width * (h + 1)]
                          for cc in range(nc) for h in range(DN_H)], axis=0)

    rt = lax.broadcasted_iota(jnp.int32, (tt, tt), 0)
    ct = lax.broadcasted_iota(jnp.int32, (tt, tt), 1)
    chunk_tri = jnp.where((rt >= ct) & (rt // c == ct // c), 1.0, 0.0).astype(BF16)
    gc_all = _dot_exact_lhs(chunk_tri, g_all)

    q = stack(act, 0, DN_DK)
    k = stack(act, DN_QK, DN_DK)
    v = stack(act, 2 * DN_QK, DN_DV)
    q = q * lax.rsqrt(jnp.sum(q * q, axis=-1, keepdims=True) + EPS) * (DN_DK ** -0.5)
    k = k * lax.rsqrt(jnp.sum(k * k, axis=-1, keepdims=True) + EPS)
    beta = stack(beta_all, 0, 1)
    gc = stack(gc_all, DN_H, 1)
    gc_t = [jnp.transpose(gc_all[cc * c:(cc + 1) * c, :]) for cc in range(nc)]
    gc_row = jnp.stack([gc_t[cc][DN_H + h:DN_H + h + 1, :] for cc in range(nc) for h in range(DN_H)], axis=0)
    decay = jnp.where(causal, jnp.exp(jnp.where(causal, gc - gc_row, 0.0)), 0.0)
    kb, vb = k * beta, v * beta
    low = jnp.where(strict, _dot(kb, k, hp, _BNT) * decay, 0.0)
    tinv = _tri_inverse(low, eye)
    eg = jnp.exp(gc)
    sol = _dot(tinv, jnp.concatenate([vb, kb * eg], axis=-1), True, _BNN)
    u, w = sol[:, :, :DN_DV], sol[:, :, DN_DV:]
    qk = _dot(q, k, hp, _BNT) * decay
    gc_last = gc[:, c - 1:c, :]
    qd = q * eg
    kd = k * jnp.exp(gc_last - gc)
    gl = jnp.exp(gc_last)
    ng = ng_ref[...]

    s = s_scr[...]
    for cc in range(nc):
        hs = slice(cc * DN_H, (cc + 1) * DN_H)
        v_new = u[hs] - _dot(w[hs], s, hp, _BNN)
        o = _dot(qd[hs], s, hp, _BNN) + _dot(qk[hs], v_new, hp, _BNN)
        kdt = jnp.stack([jnp.transpose(kd[cc * DN_H + h]) for h in range(DN_H)], axis=0)
        s = s * gl[hs] + _dot(kdt, v_new, hp, _BNN)
        for h in range(DN_H):
            zz = z_ref[cc * c:(cc + 1) * c, DN_DV * h:DN_DV * (h + 1)]
            o_ref[cc * c:(cc + 1) * c, DN_DV * h:DN_DV * (h + 1)] = _rms(o[h], ng) * _silu(zz)
    s_scr[...] = s

    @pl.when(t == pl.num_programs(1) - 1)
    def _():
        s_out_ref[...] = s_scr[...]


def _deltanet(qkv, z, ba, conv0, s0, conv_w, a_log, dt_bias, norm_g, *, batch, t_pad, t_real, tt, hp):
    n = qkv.shape[0]
    assert n == batch * t_pad and t_pad % tt == 0 and tt % min(DN_CHUNK, tt) == 0 and tt % 16 == 0
    nt = t_pad // tt
    conv0p = jnp.pad(conv0, ((0, 0), (8 - (CONV_W - 1), 0), (0, 0)))
    cwp = jnp.pad(conv_w, ((0, 8 - CONV_W), (0, 0)))
    lane_vec = lambda a: jnp.pad(a, (DN_H, LANES - 2 * DN_H)).reshape(1, LANES)
    row = lambda w: pl.BlockSpec((tt, w), lambda b, t: (b * nt + t, 0))
    const = lambda shape: pl.BlockSpec(shape, lambda b, t: (0,) * len(shape))
    a_out, s_out = pl.pallas_call(
        functools.partial(_deltanet_kernel, tt=tt, t_real=None if t_real == t_pad else t_real, hp=hp),
        grid=(batch, nt),
        in_specs=[row(DN_CH), row(DN_QK), row(LANES),
                  pl.BlockSpec((None, 8, DN_CH), lambda b, t: (b, 0, 0)),
                  pl.BlockSpec((None, DN_H, DN_DK, DN_DV), lambda b, t: (b, 0, 0, 0)),
                  const((8, DN_CH)), const((1, LANES)), const((1, LANES)), const((1, DN_DV))],
        out_specs=[row(DN_QK), pl.BlockSpec((None, DN_H, DN_DK, DN_DV), lambda b, t: (b, 0, 0, 0))],
        out_shape=[jax.ShapeDtypeStruct((n, DN_QK), F32), jax.ShapeDtypeStruct(s0.shape, F32)],
        scratch_shapes=[pltpu.VMEM((tt + 8, DN_CH), F32), pltpu.VMEM((DN_H, DN_DK, DN_DV), F32)],
        compiler_params=_cparams(("parallel", "arbitrary")), name="deltanet",
    )(qkv, z, ba, conv0p, s0, cwp, lane_vec(a_log), lane_vec(dt_bias), norm_g.reshape(1, DN_DV))
    return a_out, s_out


def _mix_kernel(x_ref, a_ref, b_ref, ga_ref, gb_ref, wa_ref, wb_ref, wo_ref, fn_ref, *refs, hp, n_experts, gated):
    if n_experts:
        wr_ref, x1_ref, h_ref, comb_ref = refs
    else:
        x1_ref, h_ref = refs
    a = _dot(a_ref[...], wa_ref[...], hp)
    b = _dot(b_ref[...], wb_ref[...], hp)
    gate = (lambda r: r[...].astype(F32)) if gated else (lambda r: jax.nn.sigmoid(r[...]))
    mixed = gate(ga_ref) * a + gate(gb_ref) * b
    x1 = x_ref[...] + _dot(mixed, wo_ref[...], hp)
    x1_ref[...] = x1
    h = _rms(x1, fn_ref[...])
    h_ref[...] = h.astype(h_ref.dtype)
    if n_experts:
        logits = _dot(h, wr_ref[...], True)
        lane = lax.broadcasted_iota(jnp.int32, logits.shape, 1).astype(F32)
        lg = jnp.where(lane < n_experts, logits, -jnp.inf)
        m1 = jnp.max(lg, axis=-1, keepdims=True)
        i1 = jnp.min(jnp.where(lg == m1, lane, float(LANES)), axis=-1, keepdims=True)
        lg2 = jnp.where(lane == i1, -jnp.inf, lg)
        m2 = jnp.max(lg2, axis=-1, keepdims=True)
        i2 = jnp.min(jnp.where(lg2 == m2, lane, float(LANES)), axis=-1, keepdims=True)
        e2 = jnp.exp(m2 - m1)
        den = 1.0 + e2
        comb_ref[...] = jnp.where(lane == i1, 1.0 / den, 0.0) + jnp.where(lane == i2, e2 / den, 0.0)


def _mix(x, a_out, b_out, g_raw, wa, wb, wo, fn, w_router, *, hp, tm, h_dtype, gated):
    n, d = x.shape
    assert n % tm == 0
    n_experts = 0 if w_router is None else w_router.shape[1]
    row = lambda w, c=0: pl.BlockSpec((tm, w), lambda i, c=c: (i, c))
    const = lambda shape: pl.BlockSpec(shape, lambda i: (0, 0))
    in_specs = [row(d), row(a_out.shape[1]), row(b_out.shape[1]), row(d, 0), row(d, 1),
                const(wa.shape), const(wb.shape), const(wo.shape), const((1, d))]
    args = [x, a_out, b_out, g_raw, g_raw, wa, wb, wo, fn.reshape(1, d)]
    out_specs = [row(d), row(d)]
    out_shape = [jax.ShapeDtypeStruct((n, d), F32), jax.ShapeDtypeStruct((n, d), h_dtype)]
    if n_experts:
        in_specs.append(const((d, LANES)))
        args.append(jnp.pad(w_router, ((0, 0), (0, LANES - n_experts))))
        out_specs.append(row(LANES))
        out_shape.append(jax.ShapeDtypeStruct((n, LANES), F32))
    return pl.pallas_call(
        functools.partial(_mix_kernel, hp=hp, n_experts=n_experts, gated=gated),
        grid=(n // tm,), in_specs=in_specs, out_specs=out_specs, out_shape=out_shape,
        compiler_params=_cparams(("parallel",)), name="mix_out",
    )(*args)


def _ffn_kernel(x1_ref, h_ref, *refs, hp, use_comb):
    if use_comb:
        comb_ref, wg_ref, wu_ref, wd_ref, o_ref = refs
    else:
        wg_ref, wu_ref, wd_ref, o_ref = refs
    e, f = pl.program_id(1), pl.program_id(2)

    @pl.when((e == 0) & (f == 0))
    def _():
        o_ref[...] = x1_ref[...]

    h = h_ref[...]
    act = _silu(_dot(h, wg_ref[...], hp)) * _dot(h, wu_ref[...], hp)
    y = _dot(act, wd_ref[...], hp)
    if use_comb:
        comb = comb_ref[...]
        lane = lax.broadcasted_iota(jnp.int32, comb.shape, 1)
        y = y * jnp.sum(jnp.where(lane == e, comb, 0.0), axis=-1, keepdims=True)
    o_ref[...] += y


def _ffn(x1, h, comb, w_gu, w_down, *, hp, tm, tf):
    n, d = x1.shape
    n_e, f_dim = w_down.shape[0], w_down.shape[1]
    assert n % tm == 0 and f_dim % tf == 0
    nf = f_dim // tf
    row = lambda w: pl.BlockSpec((tm, w), lambda i, e, f: (i, 0))
    in_specs = [row(d), row(d)]
    args = [x1, h]
    if comb is not None:
        in_specs.append(row(LANES))
        args.append(comb)
    in_specs += [pl.BlockSpec((None, d, tf), lambda i, e, f: (e, 0, f)),
                 pl.BlockSpec((None, d, tf), lambda i, e, f: (e, 0, f + nf)),
                 pl.BlockSpec((None, tf, d), lambda i, e, f: (e, f, 0))]
    args += [w_gu, w_gu, w_down]
    return pl.pallas_call(
        functools.partial(_ffn_kernel, hp=hp, use_comb=comb is not None),
        grid=(n // tm, n_e, nf), in_specs=in_specs, out_specs=row(d),
        out_shape=jax.ShapeDtypeStruct((n, d), F32),
        compiler_params=_cparams(("parallel", "arbitrary", "arbitrary")), name="ffn",
    )(*args)


def _split_w_in(w):
    o = [0]
    for s in (DN_CH, DN_QK, DN_H, DN_H, DA_W, DA_W, DA_W):
        o.append(o[-1] + s)
    ba = jnp.pad(w[:, o[2]:o[4]], ((0, 0), (0, LANES - 2 * DN_H)))
    return [w[:, o[0]:o[1]], w[:, o[1]:o[2]], ba, w[:, o[4]:o[5]], w[:, o[5]:o[6]], w[:, o[6]:o[7]], w[:, o[7]:]]


def _trunk(x, pos0, conv0, ssm0, paged, p, *, hp):
    b, t, d = x.shape
    n = b * t
    depth = p['w_in'].shape[0]
    wdt = F32 if hp else BF16
    cast = lambda w: w.astype(wdt)
    x2 = x.reshape(n, d)
    tm = min(256, n)
    cos, sin = _rope_tables(pos0 + jnp.arange(t, dtype=jnp.int32))
    chunk = DN_CHUNK if t >= DN_CHUNK else -(-t // 16) * 16
    t_pad = -(-t // chunk) * chunk
    tt = min(512, t_pad)
    k_rows, v_rows, ssm_out, conv_out = [], [], [], []
    kv_rows = None
    for l in range(depth):
        lam_init = 0.8 - 0.6 * math.exp(-0.3 * l)
        ws = [cast(w) for w in _split_w_in(p['w_in'][l])]
        lams = (p['lambda_q1'][l], p['lambda_k1'][l], p['lambda_q2'][l], p['lambda_k2'][l])
        if paged is None:
            qkv_pre, z, ba, g_raw, q_s, k_bf, v_bf, k_t, v4 = _in_proj(
                x2, p['attn_norm'][l], ws, p['q_norm'][l], p['k_norm'][l], cos, sin, kv_rows,
                batch=b, tm=tm, layer=l, depth=depth)
            kv_rows = (k_t, v4)
            b_out = _flash(q_s, k_bf, v_bf, lams, p['subln'][l], batch=b, blk=min(512, t), lam_init=lam_init)
        else:
            outs = []
            for grp in (ws[0:3], ws[3:6], ws[6:7]):
                outs += _norm_proj(x2, p['attn_norm'][l], grp, hp=hp, tm=tm)
            qkv_pre, z, ba, q_raw, k_raw, v_raw, g_raw = outs
            q_s, k_rot, _, _ = _qk_prep(q_raw, k_raw, v_raw, p['q_norm'][l], p['k_norm'][l], cos, sin,
                                        hp=hp, tm=tm, q_dtype=F32)
            kpool, vpool, page_table = paged
            b_out = _decode_attn(q_s, k_rot, v_raw, kpool, vpool, page_table, l, lams, p['subln'][l],
                                 lam_init=lam_init, group=math.gcd(32, page_table.shape[1]))
            k_rows.append(k_rot.reshape(b, t, 2 * DA_H, DA_DH))
            v_rows.append(v_raw.reshape(b, t, DA_H, 2 * DA_DH))

        def pad_t(a):
            if t_pad == t:
                return a
            return jnp.pad(a.reshape(b, t, -1), ((0, 0), (0, t_pad - t), (0, 0))).reshape(b * t_pad, -1)

        a_out, s_new = _deltanet(pad_t(qkv_pre), pad_t(z), pad_t(ba), conv0[l], ssm0[l], p['conv_w'][l],
                                 p['a_log'][l], p['dt_bias'][l], p['dn_norm'][l],
                                 batch=b, t_pad=t_pad, t_real=t, tt=tt, hp=hp)
        if t_pad != t:
            a_out = a_out.reshape(b, t_pad, -1)[:, :t].reshape(n, -1)

        moe = l % 2 == 1
        mixed = _mix(x2, a_out, b_out, g_raw, cast(p['w_a_proj'][l]), cast(p['w_b_proj'][l]), cast(p['w_out'][l]),
                     p['ffn_norm'][l], p['w_router'][l // 2] if moe else None, hp=hp, tm=min(512, n), h_dtype=wdt,
                     gated=paged is None)
        if moe:
            x1, h2, comb = mixed
            x2 = _ffn(x1, h2, comb, cast(p['w_exp_gu'][l // 2]), cast(p['w_exp_down'][l // 2]),
                      hp=hp, tm=min(1024, n), tf=p['w_exp_down'].shape[2])
        else:
            x1, h2 = mixed
            x2 = _ffn(x1, h2, None, cast(p['w_ffn_gu'][l // 2][None]), cast(p['w_ffn_down'][l // 2][None]),
                      hp=hp, tm=min(1024, n), tf=p['w_ffn_down'].shape[1] // 2)

        ssm_out.append(s_new)
        tail = qkv_pre.reshape(b, t, DN_CH)[:, max(t - (CONV_W - 1), 0):]
        conv_out.append(jnp.concatenate([conv0[l], tail], axis=1)[:, -(CONV_W - 1):])
    if paged is None:
        k_t, v4 = kv_rows
        k_all = jnp.transpose(k_t.reshape(depth, b, 2 * DA_H, DA_DH, t), (0, 1, 4, 2, 3))
        v_all = v4.reshape(depth, b, t, DA_H, 2 * DA_DH)
    else:
        k_all, v_all = jnp.stack(k_rows), jnp.stack(v_rows)
    return x2.reshape(b, t, d), k_all, v_all, jnp.stack(ssm_out), jnp.stack(conv_out)


def kernel(x_prompt, x_sample, cache_k, cache_v, state_ssm, state_conv, page_table, attn_norm, w_in, conv_w,
           a_log, dt_bias, dn_norm, q_norm, k_norm, lambda_q1, lambda_k1, lambda_q2, lambda_k2, subln,
           w_a_proj, w_b_proj, w_out, ffn_norm, w_ffn_gu, w_ffn_down, w_router, w_exp_gu, w_exp_down):
    p = dict(attn_norm=attn_norm, w_in=w_in, conv_w=conv_w, a_log=a_log, dt_bias=dt_bias, dn_norm=dn_norm,
             q_norm=q_norm, k_norm=k_norm, lambda_q1=lambda_q1, lambda_k1=lambda_k1, lambda_q2=lambda_q2,
             lambda_k2=lambda_k2, subln=subln, w_a_proj=w_a_proj, w_b_proj=w_b_proj, w_out=w_out,
             ffn_norm=ffn_norm, w_ffn_gu=w_ffn_gu, w_ffn_down=w_ffn_down, w_router=w_router,
             w_exp_gu=w_exp_gu, w_exp_down=w_exp_down)
    depth = w_in.shape[0]
    bp = x_prompt.shape[0]
    assert x_sample.shape[1] == 1, "the decode path handles one new token per sequence"
    assert cache_k.shape[2] == PAGE
    conv0 = jnp.zeros((depth, bp, CONV_W - 1, DN_CH), F32)
    ssm0 = jnp.zeros((depth, bp, DN_H, DN_DK, DN_DV), F32)
    y_p, k_p, v_p, ssm_p, conv_p = _trunk(x_prompt, 0, conv0, ssm0, None, p, hp=False)

    past_len = page_table.shape[1] * PAGE
    n_pool = cache_k.shape[1]
    kpool = jnp.transpose(cache_k, (0, 1, 3, 4, 2))
    vpool = cache_v.reshape(depth, n_pool, PAGE * DA_H, 2 * DA_DH)
    y_s, k_s, v_s, ssm_s, conv_s = _trunk(x_sample, past_len, state_conv, state_ssm, (kpool, vpool, page_table), p,
                                          hp=True)
    return (y_p, y_s, k_p, v_p, ssm_p, conv_p, k_s, v_s, ssm_s, conv_s)
```

```python
import functools
import math

import jax
import jax.numpy as jnp
from jax import lax
from jax.experimental import pallas as pl
from jax.experimental.pallas import tpu as pltpu

F32 = jnp.float32
BF16 = jnp.bfloat16
EPS = 1e-6
LANES = 128
VMEM_LIMIT = 56 * 1024 * 1024

DN_H, DN_DK, DN_DV = 4, 128, 128
DN_QK = DN_H * DN_DK
DN_CH = 3 * DN_QK
CONV_W = 4
DN_CHUNK = 64
DA_H, DA_DH = 4, 64
DA_W = 2 * DA_H * DA_DH
ROPE_THETA = 10000.0
PAGE = 128

_NN = (((1,), (0,)), ((), ()))
_NT = (((1,), (1,)), ((), ()))
_BNN = (((2,), (1,)), ((0,), (0,)))
_BNT = (((2,), (2,)), ((0,), (0,)))
assert 2 * DA_DH == LANES and DN_DV == LANES


def _cparams(sem):
    return pltpu.CompilerParams(dimension_semantics=sem, vmem_limit_bytes=VMEM_LIMIT)


def _mm(a, b, dims=_NN):
    return lax.dot_general(a, b, dims, preferred_element_type=F32)


def _split(a):
    if a.dtype == BF16:
        return a, None
    hi = a.astype(BF16)
    lo = (a - hi.astype(F32)).astype(BF16)
    return hi, lo


def _dot(a, b, hp, dims=_NN):
    if not hp:
        return _mm(a.astype(BF16), b.astype(BF16), dims)
    ah, al = _split(a)
    bh, bl = _split(b)
    out = _mm(ah, bh, dims)
    if al is not None:
        out = out + _mm(al, bh, dims)
    if bl is not None:
        out = out + _mm(ah, bl, dims)
    return out


def _split3(a):
    hi = a.astype(BF16)
    r = a - hi.astype(F32)
    mid = r.astype(BF16)
    lo = (r - mid.astype(F32)).astype(BF16)
    return hi, mid, lo


def _dot_exact_lhs(a_bf16, b, dims=_NN):
    hi, mid, lo = _split3(b)
    return _mm(a_bf16, hi, dims) + (_mm(a_bf16, mid, dims) + _mm(a_bf16, lo, dims))


def _dot_exact_rhs(a, b_bf16):
    hi, mid, lo = _split3(a)
    return _mm(hi, b_bf16) + (_mm(mid, b_bf16) + _mm(lo, b_bf16))


def _rms(x, g):
    return x * lax.rsqrt(jnp.mean(x * x, axis=-1, keepdims=True) + EPS) * g


def _silu(x):
    return x * jax.nn.sigmoid(x)


def _norm_proj_kernel(x_ref, g_ref, *refs, n_w, hp):
    w_refs, o_refs = refs[:n_w], refs[n_w:]
    h = _rms(x_ref[...], g_ref[...])
    if not hp:
        h = h.astype(BF16)
    for w_ref, o_ref in zip(w_refs, o_refs):
        o_ref[...] = _dot(h, w_ref[...], hp).astype(o_ref.dtype)


def _norm_proj(x, g, ws, *, hp, tm):
    n, d = x.shape
    assert n % tm == 0
    in_specs = [pl.BlockSpec((tm, d), lambda i: (i, 0)), pl.BlockSpec((1, d), lambda i: (0, 0))]
    in_specs += [pl.BlockSpec(w.shape, lambda i: (0, 0)) for w in ws]
    out_specs = [pl.BlockSpec((tm, w.shape[1]), lambda i: (i, 0)) for w in ws]
    out_shape = [jax.ShapeDtypeStruct((n, w.shape[1]), F32) for w in ws]
    return pl.pallas_call(
        functools.partial(_norm_proj_kernel, n_w=len(ws), hp=hp),
        grid=(n // tm,), in_specs=in_specs, out_specs=out_specs, out_shape=out_shape,
        compiler_params=_cparams(("parallel",)), name="norm_proj",
    )(x, g.reshape(1, d), *ws)


def _qk_prep_kernel(q_ref, k_ref, v_ref, qn_ref, kn_ref, cos_ref, sin_ref, bd_ref,
                    qo_ref, ko_ref, kb_ref, vb_ref, *, hp, q_scale):
    tm = q_ref.shape[0]
    lane = lax.broadcasted_iota(jnp.int32, (tm, DA_W), 1)
    first_half = (lane % DA_DH) < (DA_DH // 2)
    bd = bd_ref[...]
    cos, sin = cos_ref[...], sin_ref[...]
    q = _norm_rope(q_ref[...], qn_ref[...], bd, cos, sin, first_half, hp) * q_scale
    k = _norm_rope(k_ref[...], kn_ref[...], bd, cos, sin, first_half, hp)
    qo_ref[...] = q.astype(qo_ref.dtype)
    ko_ref[...] = k
    kb_ref[...] = k.astype(BF16)
    vb_ref[...] = v_ref[...].astype(BF16)


def _rope_tables(pos):
    half = DA_DH // 2
    inv = ROPE_THETA ** (-jnp.arange(half, dtype=F32) / half)
    ang = pos.astype(F32)[:, None] * inv[None, :]
    cos, sin = jnp.cos(ang), jnp.sin(ang)
    cos = jnp.tile(jnp.concatenate([cos, cos], axis=-1), (1, 2 * DA_H))
    sin = jnp.tile(jnp.concatenate([-sin, sin], axis=-1), (1, 2 * DA_H))
    return cos, sin


def _qk_prep(q_raw, k_raw, v_raw, qn, kn, cos, sin, *, hp, tm, q_dtype):
    n = q_raw.shape[0]
    t = cos.shape[0]
    if t % tm:
        cos, sin = jnp.tile(cos, (n // t, 1)), jnp.tile(sin, (n // t, 1))
        t = n
    assert n % tm == 0 and t % tm == 0
    nt = t // tm
    idx = jnp.arange(DA_W) // DA_DH
    bd = (idx[:, None] == idx[None, :]).astype(BF16)
    row = pl.BlockSpec((tm, DA_W), lambda i: (i, 0))
    vec = pl.BlockSpec((1, DA_W), lambda i: (0, 0))
    tab = pl.BlockSpec((tm, DA_W), lambda i: (i % nt, 0))
    return pl.pallas_call(
        functools.partial(_qk_prep_kernel, hp=hp, q_scale=DA_DH ** -0.5),
        grid=(n // tm,),
        in_specs=[row, row, row, vec, vec, tab, tab, pl.BlockSpec((DA_W, DA_W), lambda i: (0, 0))],
        out_specs=[row, row, row, row],
        out_shape=[jax.ShapeDtypeStruct((n, DA_W), q_dtype), jax.ShapeDtypeStruct((n, DA_W), F32),
                   jax.ShapeDtypeStruct((n, DA_W), BF16), jax.ShapeDtypeStruct((n, DA_W), BF16)],
        compiler_params=_cparams(("parallel",)), name="qk_prep",
    )(q_raw, k_raw, v_raw, jnp.tile(qn, 2 * DA_H).reshape(1, DA_W), jnp.tile(kn, 2 * DA_H).reshape(1, DA_W), cos, sin, bd)


def _norm_rope(x, g, bd, cos, sin, first_half, hp):
    sq = x * x
    ss = _dot_exact_rhs(sq, bd) if hp else _mm(sq.astype(BF16), bd)
    y = x * lax.rsqrt(ss * (1.0 / DA_DH) + EPS) * g
    partner = jnp.where(first_half, pltpu.roll(y, DA_W - DA_DH // 2, axis=1), pltpu.roll(y, DA_DH // 2, axis=1))
    return y * cos + partner * sin


def _in_proj_kernel(x_ref, g_ref, wqkv, wz, wba, wq, wk, wv, wg, qn_ref, kn_ref, cos_ref, sin_ref, bd_ref, *refs):
    qkv_o, z_o, ba_o, g_o, q_o, kb_o, vb_o, kt_o, v4_o = refs[-9:]
    tm = x_ref.shape[0]
    h = _rms(x_ref[...], g_ref[...]).astype(BF16)
    qkv_o[...] = _mm(h, wqkv[...])
    z_o[...] = _mm(h, wz[...])
    ba_o[...] = _mm(h, wba[...])
    g_o[...] = jax.nn.sigmoid(_mm(h, wg[...])).astype(BF16)
    lane = lax.broadcasted_iota(jnp.int32, (tm, DA_W), 1)
    first_half = (lane % DA_DH) < (DA_DH // 2)
    bd, cos, sin = bd_ref[...], cos_ref[...], sin_ref[...]
    q = _norm_rope(_mm(h, wq[...]), qn_ref[...], bd, cos, sin, first_half, False) * (DA_DH ** -0.5)
    k = _norm_rope(_mm(h, wk[...]), kn_ref[...], bd, cos, sin, first_half, False)
    v = _mm(h, wv[...])
    q_o[...] = q.astype(BF16)
    kb_o[...] = k.astype(BF16)
    vb_o[...] = v.astype(BF16)
    kt_o[...] = jnp.transpose(k)
    for hh in range(DA_H):
        v4_o[pl.ds(hh, tm, stride=DA_H), :] = v[:, 2 * DA_DH * hh:2 * DA_DH * (hh + 1)]


def _in_proj(x, g, ws, qn, kn, cos, sin, kv_rows, *, batch, tm, layer, depth):
    n, d = x.shape
    t = n // batch
    assert n % tm == 0 and t % tm == 0
    nt = t // tm
    idx = jnp.arange(DA_W) // DA_DH
    bd = (idx[:, None] == idx[None, :]).astype(BF16)
    row = lambda w, dt=F32: (pl.BlockSpec((tm, w), lambda i: (i, 0)), jax.ShapeDtypeStruct((n, w), dt))
    const = lambda shape: pl.BlockSpec(shape, lambda i: (0, 0))
    tab = pl.BlockSpec((tm, DA_W), lambda i: (i % nt, 0))
    outs = [row(DN_CH), row(DN_QK), row(LANES), row(ws[6].shape[1], BF16), row(DA_W, BF16), row(DA_W, BF16), row(DA_W, BF16),
            (pl.BlockSpec((None, None, DA_W, tm), lambda i: (layer, i // nt, 0, i % nt)),
             jax.ShapeDtypeStruct((depth, batch, DA_W, t), F32)),
            (pl.BlockSpec((None, tm * DA_H, 2 * DA_DH), lambda i: (layer, i, 0)),
             jax.ShapeDtypeStruct((depth, n * DA_H, 2 * DA_DH), F32))]
    in_specs = [pl.BlockSpec((tm, d), lambda i: (i, 0)), const((1, d))] + [const(w.shape) for w in ws]
    in_specs += [const((1, DA_W)), const((1, DA_W)), tab, tab, const((DA_W, DA_W))]
    args = [x, g.reshape(1, d), *ws, jnp.tile(qn, 2 * DA_H).reshape(1, DA_W), jnp.tile(kn, 2 * DA_H).reshape(1, DA_W),
            cos, sin, bd]
    aliases = {}
    if kv_rows is not None:
        aliases = {len(args): len(outs) - 2, len(args) + 1: len(outs) - 1}
        in_specs += [pl.BlockSpec(memory_space=pl.ANY)] * 2
        args += list(kv_rows)
    return pl.pallas_call(
        _in_proj_kernel, grid=(n // tm,), in_specs=in_specs,
        out_specs=[o[0] for o in outs], out_shape=[o[1] for o in outs], input_output_aliases=aliases,
        compiler_params=_cparams(("parallel",)), name="in_proj",
    )(*args)


def _lambda(lq1, lk1, lq2, lk2, lam_init):
    return (jnp.exp(jnp.sum(lq1[...] * lk1[...], axis=-1, keepdims=True))
            - jnp.exp(jnp.sum(lq2[...] * lk2[...], axis=-1, keepdims=True)) + lam_init)


def _flash_kernel(qi_ref, kj_ref, q_ref, k_ref, v_ref, lq1, lk1, lq2, lk2, sg_ref, o_ref, qs_scr, m_scr, l_scr,
                  acc_scr, s_scr, p_scr, al_scr, *, blk, lam_init):
    i, j = qi_ref[pl.program_id(1)], kj_ref[pl.program_id(1)]
    hw = 2 * DA_DH
    rb = min(64, blk)

    @pl.when(j == 0)
    def _():
        q = q_ref[...]
        lane = lax.broadcasted_iota(jnp.int32, (blk, hw), 1)
        zero = jnp.zeros((blk, hw), q.dtype)
        for h in range(DA_H):
            qh = q[:, hw * h:hw * (h + 1)]
            qs_scr[h, 0:blk, :] = jnp.where(lane < DA_DH, qh, zero)
            qs_scr[h, blk:2 * blk, :] = jnp.where(lane >= DA_DH, qh, zero)
        m_scr[...] = jnp.full(m_scr.shape, -jnp.inf, F32)
        l_scr[...] = jnp.zeros(l_scr.shape, F32)
        acc_scr[...] = jnp.zeros(acc_scr.shape, F32)

    def step(diag):
        if diag:
            r = lax.broadcasted_iota(jnp.int32, (2 * blk, blk), 0)
            c = lax.broadcasted_iota(jnp.int32, (2 * blk, blk), 1)
            keep = c <= jnp.where(r >= blk, r - blk, r)
        for h in range(DA_H):
            s = _mm(qs_scr[h], k_ref[:, hw * h:hw * (h + 1)], _NT)
            if diag:
                s = jnp.where(keep, s, -jnp.inf)
            s_scr[h] = s
            m_prev = m_scr[h]
            m_new = jnp.maximum(m_prev, jnp.max(s, axis=-1, keepdims=True))
            m_scr[h] = m_new
            al_scr[h] = jnp.exp(m_prev - m_new)
            for t in range(2 * blk // rb):
                rows = slice(t * rb, (t + 1) * rb)
                m_rows = m_scr[h, rows, :]
                p = jnp.exp(s_scr[h, rows, :] - jnp.concatenate([m_rows] * (blk // LANES), axis=1))
                l_scr[h, rows, :] = al_scr[h, rows, :] * l_scr[h, rows, :] + jnp.sum(p, axis=-1, keepdims=True)
                p_scr[h, rows, :] = p.astype(BF16)
            acc_scr[h] = al_scr[h] * acc_scr[h] + _mm(p_scr[h], v_ref[:, hw * h:hw * (h + 1)])

    @pl.when(j < i)
    def _():
        step(False)

    @pl.when(j == i)
    def _():
        step(True)
        lam = _lambda(lq1, lk1, lq2, lk2, lam_init)
        for h in range(DA_H):
            acc = acc_scr[h]
            l = l_scr[h]
            o = acc[0:blk] / l[0:blk] - lam * (acc[blk:] / l[blk:])
            o_ref[:, hw * h:hw * (h + 1)] = _rms(o, sg_ref[...]) * (1.0 - lam_init)


def _flash(q, k, v, lams, subln, *, batch, blk, lam_init):
    n = q.shape[0]
    nq = n // batch // blk
    hw = 2 * DA_DH
    pairs = [(i, j) for i in range(nq) for j in range(i + 1)]
    qi = jnp.array([ij[0] for ij in pairs], jnp.int32)
    kj = jnp.array([ij[1] for ij in pairs], jnp.int32)
    qspec = pl.BlockSpec((blk, DA_W), lambda b, s, qi, kj: (b * nq + qi[s], 0))
    kspec = pl.BlockSpec((blk, DA_W), lambda b, s, qi, kj: (b * nq + kj[s], 0))
    lspec = pl.BlockSpec((1, DA_DH), lambda b, s, qi, kj: (0, 0))
    return pl.pallas_call(
        functools.partial(_flash_kernel, blk=blk, lam_init=lam_init),
        grid_spec=pltpu.PrefetchScalarGridSpec(
            num_scalar_prefetch=2, grid=(batch, len(pairs)),
            in_specs=[qspec, kspec, kspec, lspec, lspec, lspec, lspec,
                      pl.BlockSpec((1, hw), lambda b, s, qi, kj: (0, 0))],
            out_specs=qspec,
            scratch_shapes=[pltpu.VMEM((DA_H, 2 * blk, hw), BF16), pltpu.VMEM((DA_H, 2 * blk, LANES), F32),
                            pltpu.VMEM((DA_H, 2 * blk, LANES), F32), pltpu.VMEM((DA_H, 2 * blk, hw), F32),
                            pltpu.VMEM((DA_H, 2 * blk, blk), F32), pltpu.VMEM((DA_H, 2 * blk, blk), BF16),
                            pltpu.VMEM((DA_H, 2 * blk, LANES), F32)]),
        out_shape=jax.ShapeDtypeStruct((n, DA_W), F32),
        compiler_params=_cparams(("parallel", "arbitrary")), name="flash_diff_attn",
    )(qi, kj, q, k, v, *[x.reshape(1, DA_DH) for x in lams], subln.reshape(1, hw))


def _decode_kernel(pt_ref, qbd_ref, q8_ref, kn_ref, vn_ref, lq1, lk1, lq2, lk2, sg_ref, *refs,
                   group, lam_init):
    k_refs, v_refs = refs[:group], refs[group:2 * group]
    o_ref, m_scr, l_scr, acc_scr = refs[2 * group:]
    s_idx = pl.program_id(1)
    n_maps = 2 * DA_H

    @pl.when(s_idx == 0)
    def _():
        s0 = jnp.sum(q8_ref[...] * kn_ref[...], axis=-1, keepdims=True)
        m_scr[...] = jnp.broadcast_to(s0, (n_maps, LANES))
        l_scr[...] = jnp.ones((n_maps, LANES), F32)
        for h in range(DA_H):
            acc_scr[h] = jnp.broadcast_to(vn_ref[h:h + 1, :], (n_maps, 2 * DA_DH))

    q_hi, q_lo = _split(qbd_ref[...])
    a16 = jnp.concatenate([q_hi, q_lo], axis=0)

    s_list = []
    for g in range(group):
        kt = k_refs[g][...].reshape(n_maps * DA_DH, PAGE)
        k_hi, k_lo = _split(kt)
        r = _mm(a16, k_hi) + _mm(a16, k_lo)
        s_list.append(r[0:n_maps] + r[n_maps:])
    m_prev = m_scr[...][:, :1]
    m_cur = s_list[0].max(axis=-1, keepdims=True)
    for s in s_list[1:]:
        m_cur = jnp.maximum(m_cur, s.max(axis=-1, keepdims=True))
    m_new = jnp.maximum(m_prev, m_cur)
    alpha = jnp.exp(m_prev - m_new)
    l_new = alpha * l_scr[...][:, :1]
    acc = [alpha * acc_scr[h] for h in range(DA_H)]
    for g in range(group):
        p = jnp.exp(s_list[g] - m_new)
        l_new = l_new + jnp.sum(p, axis=-1, keepdims=True)
        p_hi, p_lo = _split(p)
        p16 = jnp.concatenate([p_hi, p_lo], axis=0)
        for h in range(DA_H):
            v_hi, v_lo = _split(v_refs[g][pl.ds(h, PAGE, stride=DA_H), :])
            r = _mm(p16, v_hi) + _mm(p16, v_lo)
            acc[h] = acc[h] + (r[0:n_maps] + r[n_maps:])
    m_scr[...] = jnp.broadcast_to(m_new, (n_maps, LANES))
    l_scr[...] = jnp.broadcast_to(l_new, (n_maps, LANES))
    for h in range(DA_H):
        acc_scr[h] = acc[h]

    @pl.when(s_idx == pl.num_programs(1) - 1)
    def _():
        lam = _lambda(lq1, lk1, lq2, lk2, lam_init)
        for h in range(DA_H):
            o1 = acc[h][2 * h:2 * h + 1] / l_new[2 * h:2 * h + 1]
            o2 = acc[h][2 * h + 1:2 * h + 2] / l_new[2 * h + 1:2 * h + 2]
            o_ref[h:h + 1, :] = _rms(o1 - lam * o2, sg_ref[...]) * (1.0 - lam_init)


def _decode_attn(q, k_new, v_new, kpool, vpool, page_table, layer, lams, subln, *, lam_init, group):
    s, n_pages = page_table.shape
    assert n_pages % group == 0
    hw = 2 * DA_DH
    n_maps = 2 * DA_H
    q8 = q.reshape(s, n_maps, DA_DH)
    qbd = (q8[:, :, None, :] * jnp.eye(n_maps, dtype=F32)[None, :, :, None]).reshape(s, n_maps, DA_W)
    kn8 = k_new.reshape(s, n_maps, DA_DH)
    vn4 = v_new.reshape(s, DA_H, hw)

    small = lambda shape: pl.BlockSpec((None,) + shape, lambda b, t, pt: (b, 0, 0))
    lspec = pl.BlockSpec((1, DA_DH), lambda b, t, pt: (0, 0))
    in_specs = [small((n_maps, DA_W)), small((n_maps, DA_DH)), small((n_maps, DA_DH)), small((DA_H, hw)),
                lspec, lspec, lspec, lspec, pl.BlockSpec((1, hw), lambda b, t, pt: (0, 0))]
    in_specs += [pl.BlockSpec((None, None, n_maps, DA_DH, PAGE),
                              lambda b, t, pt, g=g: (layer, pt[b, t * group + g], 0, 0, 0)) for g in range(group)]
    in_specs += [pl.BlockSpec((None, None, PAGE * DA_H, hw),
                              lambda b, t, pt, g=g: (layer, pt[b, t * group + g], 0, 0)) for g in range(group)]
    out = pl.pallas_call(
        functools.partial(_decode_kernel, group=group, lam_init=lam_init),
        grid_spec=pltpu.PrefetchScalarGridSpec(
            num_scalar_prefetch=1, grid=(s, n_pages // group), in_specs=in_specs,
            out_specs=small((DA_H, hw)),
            scratch_shapes=[pltpu.VMEM((n_maps, LANES), F32), pltpu.VMEM((n_maps, LANES), F32),
                            pltpu.VMEM((DA_H, n_maps, hw), F32)]),
        out_shape=jax.ShapeDtypeStruct((s, DA_H, hw), F32),
        compiler_params=_cparams(("parallel", "arbitrary")), name="decode_diff_attn",
    )(page_table, qbd, q8, kn8, vn4, *[x.reshape(1, DA_DH) for x in lams], subln.reshape(1, hw),
      *([kpool] * group), *([vpool] * group))
    return out.reshape(s, DA_W)


def _tri_inverse(low, eye):
    a = -low
    t = eye + a
    ph, pl_ = _split(a)
    steps = int(math.log2(low.shape[-1])) - 1
    for _ in range(steps):
        p = _mm(ph, ph, _BNN) + (_mm(pl_, ph, _BNN) + _mm(ph, pl_, _BNN))
        ph, pl_ = _split(p)
        th, tl = _split(t)
        t = t + (_mm(th, ph, _BNN) + (_mm(tl, ph, _BNN) + _mm(th, pl_, _BNN)))
    return t


def _deltanet_kernel(qkv_ref, z_ref, ba_ref, conv0_ref, s0_ref, cw_ref, alog_ref, dtb_ref, ng_ref,
                     o_ref, s_out_ref, xbuf, s_scr, *, tt, t_real, hp):
    t = pl.program_id(1)
    ns = qkv_ref.shape[0]
    rows_all = ns * tt
    c = min(DN_CHUNK, tt)

    @pl.when(t == 0)
    def _():
        xbuf[:, 0:8, :] = conv0_ref[...]
        s_scr[...] = s0_ref[...].reshape(s_scr.shape)

    @pl.when(t > 0)
    def _():
        xbuf[:, 0:8, :] = xbuf[:, tt:tt + 8, :]

    xbuf[:, 8:8 + tt, :] = qkv_ref[...]
    cw = cw_ref[...]
    acts = []
    for u in range(ns):
        acc = xbuf[u, pl.ds(8 - (CONV_W - 1), tt), :] * cw[0:1]
        for jj in range(1, CONV_W):
            acc = acc + xbuf[u, pl.ds(8 - (CONV_W - 1) + jj, tt), :] * cw[jj:jj + 1]
        acts.append(_silu(acc))
    act = jnp.concatenate(acts, axis=0)

    ba = ba_ref[...].reshape(rows_all, LANES)
    beta_all = jax.nn.sigmoid(ba)
    sp_in = ba + dtb_ref[...]
    softplus = jnp.maximum(sp_in, 0.0) + jnp.log(1.0 + jnp.exp(-jnp.abs(sp_in)))
    g_all = -jnp.exp(alog_ref[...]) * softplus
    if t_real is not None:
        ridx = t * tt + lax.rem(lax.broadcasted_iota(jnp.int32, (rows_all, 1), 0), tt)
        live = ridx < t_real
        act = jnp.where(live, act, 0.0)
        beta_all = jnp.where(live, beta_all, 0.0)
        g_all = jnp.where(live, g_all, 0.0)

    ncs = tt // c
    nc = ns * ncs
    nb = nc * DN_H
    ri = lax.broadcasted_iota(jnp.int32, (nb, c, c), 1)
    ci = lax.broadcasted_iota(jnp.int32, (nb, c, c), 2)
    causal, strict = ri >= ci, ri > ci
    eye = jnp.where(ri == ci, 1.0, 0.0).astype(F32)

    def stack(arr, off, width):
        return jnp.stack([arr[cc * c:(cc + 1) * c, off + width * h:off + width * (h + 1)]
                          for cc in range(nc) for h in range(DN_H)], axis=0)

    rt = lax.broadcasted_iota(jnp.int32, (rows_all, rows_all), 0)
    ct = lax.broadcasted_iota(jnp.int32, (rows_all, rows_all), 1)
    chunk_tri = jnp.where((rt >= ct) & (rt // c == ct // c), 1.0, 0.0).astype(BF16)
    gc_all = _dot_exact_lhs(chunk_tri, g_all)

    q = stack(act, 0, DN_DK)
    k = stack(act, DN_QK, DN_DK)
    v = stack(act, 2 * DN_QK, DN_DV)
    q = q * lax.rsqrt(jnp.sum(q * q, axis=-1, keepdims=True) + EPS) * (DN_DK ** -0.5)
    k = k * lax.rsqrt(jnp.sum(k * k, axis=-1, keepdims=True) + EPS)
    beta = stack(beta_all, 0, 1)
    gc = stack(gc_all, DN_H, 1)
    gc_t = [jnp.transpose(gc_all[cc * c:(cc + 1) * c, :]) for cc in range(nc)]
    gc_row = jnp.stack([gc_t[cc][DN_H + h:DN_H + h + 1, :] for cc in range(nc) for h in range(DN_H)], axis=0)
    decay = jnp.where(causal, jnp.exp(jnp.where(causal, gc - gc_row, 0.0)), 0.0)
    kb, vb = k * beta, v * beta
    low = jnp.where(strict, _dot(kb, k, hp, _BNT) * decay, 0.0)
    tinv = _tri_inverse(low, eye)
    eg = jnp.exp(gc)
    sol = _dot(tinv, jnp.concatenate([vb, kb * eg], axis=-1), True, _BNN)
    u, w = sol[:, :, :DN_DV], sol[:, :, DN_DV:]
    qk = _dot(q, k, hp, _BNT) * decay
    gc_last = gc[:, c - 1:c, :]
    qd = q * eg
    kd = k * jnp.exp(gc_last - gc)
    gl = jnp.exp(gc_last)
    ng = ng_ref[...]

    s = s_scr[...]
    for cc in range(ncs):
        idx = [(sq * ncs + cc) * DN_H + h for sq in range(ns) for h in range(DN_H)]
        pick = lambda x: jnp.concatenate([x[(sq * ncs + cc) * DN_H:(sq * ncs + cc + 1) * DN_H] for sq in range(ns)], axis=0)
        v_new = pick(u) - _dot(pick(w), s, hp, _BNN)
        o = _dot(pick(qd), s, hp, _BNN) + _dot(pick(qk), v_new, hp, _BNN)
        kdt = jnp.stack([jnp.transpose(kd[i]) for i in idx], axis=0)
        s = s * pick(gl) + _dot(kdt, v_new, hp, _BNN)
        for sq in range(ns):
            for h in range(DN_H):
                zz = z_ref[sq, cc * c:(cc + 1) * c, DN_DV * h:DN_DV * (h + 1)]
                o_ref[sq, cc * c:(cc + 1) * c, DN_DV * h:DN_DV * (h + 1)] = _rms(o[sq * DN_H + h], ng) * _silu(zz)
    s_scr[...] = s

    @pl.when(t == pl.num_programs(1) - 1)
    def _():
        s_out_ref[...] = s_scr[...].reshape(s_out_ref.shape)


def _deltanet(qkv, z, ba, conv0, s0, conv_w, a_log, dt_bias, norm_g, *, batch, t_pad, t_real, tt, hp):
    n = qkv.shape[0]
    assert n == batch * t_pad and t_pad % tt == 0 and tt % min(DN_CHUNK, tt) == 0 and tt % 16 == 0
    nt = t_pad // tt
    conv0p = jnp.pad(conv0, ((0, 0), (8 - (CONV_W - 1), 0), (0, 0)))
    cwp = jnp.pad(conv_w, ((0, 8 - CONV_W), (0, 0)))
    lane_vec = lambda a: jnp.pad(a, (DN_H, LANES - 2 * DN_H)).reshape(1, LANES)
    ns = math.gcd(batch, 4)
    seq3 = lambda a: a.reshape(batch, t_pad, a.shape[-1])
    row = lambda w: pl.BlockSpec((ns, tt, w), lambda b, t: (b, t, 0))
    const = lambda shape: pl.BlockSpec(shape, lambda b, t: (0,) * len(shape))
    state = pl.BlockSpec((ns, DN_H, DN_DK, DN_DV), lambda b, t: (b, 0, 0, 0))
    a_out, s_out = pl.pallas_call(
        functools.partial(_deltanet_kernel, tt=tt, t_real=None if t_real == t_pad else t_real, hp=hp),
        grid=(batch // ns, nt),
        in_specs=[row(DN_CH), row(DN_QK), row(LANES),
                  pl.BlockSpec((ns, 8, DN_CH), lambda b, t: (b, 0, 0)), state,
                  const((8, DN_CH)), const((1, LANES)), const((1, LANES)), const((1, DN_DV))],
        out_specs=[row(DN_QK), state],
        out_shape=[jax.ShapeDtypeStruct((batch, t_pad, DN_QK), F32), jax.ShapeDtypeStruct(s0.shape, F32)],
        scratch_shapes=[pltpu.VMEM((ns, tt + 8, DN_CH), F32), pltpu.VMEM((ns * DN_H, DN_DK, DN_DV), F32)],
        compiler_params=_cparams(("parallel", "arbitrary")), name="deltanet",
    )(seq3(qkv), seq3(z), seq3(ba), conv0p, s0, cwp, lane_vec(a_log), lane_vec(dt_bias), norm_g.reshape(1, DN_DV))
    return a_out.reshape(n, DN_QK), s_out


def _mix_kernel(x_ref, a_ref, b_ref, ga_ref, gb_ref, wa_ref, wb_ref, wo_ref, fn_ref, *refs, hp, n_experts, gated):
    if n_experts:
        wr_ref, x1_ref, h_ref, comb_ref = refs
    else:
        x1_ref, h_ref = refs
    a = _dot(a_ref[...], wa_ref[...], hp)
    b = _dot(b_ref[...], wb_ref[...], hp)
    gate = (lambda r: r[...].astype(F32)) if gated else (lambda r: jax.nn.sigmoid(r[...]))
    mixed = gate(ga_ref) * a + gate(gb_ref) * b
    x1 = x_ref[...] + _dot(mixed, wo_ref[...], hp)
    x1_ref[...] = x1
    h = _rms(x1, fn_ref[...])
    h_ref[...] = h.astype(h_ref.dtype)
    if n_experts:
        logits = _dot(h, wr_ref[...], True)
        lane = lax.broadcasted_iota(jnp.int32, logits.shape, 1).astype(F32)
        lg = jnp.where(lane < n_experts, logits, -jnp.inf)
        m1 = jnp.max(lg, axis=-1, keepdims=True)
        i1 = jnp.min(jnp.where(lg == m1, lane, float(LANES)), axis=-1, keepdims=True)
        lg2 = jnp.where(lane == i1, -jnp.inf, lg)
        m2 = jnp.max(lg2, axis=-1, keepdims=True)
        i2 = jnp.min(jnp.where(lg2 == m2, lane, float(LANES)), axis=-1, keepdims=True)
        e2 = jnp.exp(m2 - m1)
        den = 1.0 + e2
        comb_ref[...] = jnp.where(lane == i1, 1.0 / den, 0.0) + jnp.where(lane == i2, e2 / den, 0.0)


def _mix(x, a_out, b_out, g_raw, wa, wb, wo, fn, w_router, *, hp, tm, h_dtype, gated):
    n, d = x.shape
    assert n % tm == 0
    n_experts = 0 if w_router is None else w_router.shape[1]
    row = lambda w, c=0: pl.BlockSpec((tm, w), lambda i, c=c: (i, c))
    const = lambda shape: pl.BlockSpec(shape, lambda i: (0, 0))
    in_specs = [row(d), row(a_out.shape[1]), row(b_out.shape[1]), row(d, 0), row(d, 1),
                const(wa.shape), const(wb.shape), const(wo.shape), const((1, d))]
    args = [x, a_out, b_out, g_raw, g_raw, wa, wb, wo, fn.reshape(1, d)]
    out_specs = [row(d), row(d)]
    out_shape = [jax.ShapeDtypeStruct((n, d), F32), jax.ShapeDtypeStruct((n, d), h_dtype)]
    if n_experts:
        in_specs.append(const((d, LANES)))
        args.append(jnp.pad(w_router, ((0, 0), (0, LANES - n_experts))))
        out_specs.append(row(LANES))
        out_shape.append(jax.ShapeDtypeStruct((n, LANES), F32))
    return pl.pallas_call(
        functools.partial(_mix_kernel, hp=hp, n_experts=n_experts, gated=gated),
        grid=(n // tm,), in_specs=in_specs, out_specs=out_specs, out_shape=out_shape,
        compiler_params=_cparams(("parallel",)), name="mix_out",
    )(*args)


def _ffn_kernel(x1_ref, h_ref, *refs, hp, use_comb):
    if use_comb:
        comb_ref, wg_ref, wu_ref, wd_ref, o_ref = refs
    else:
        wg_ref, wu_ref, wd_ref, o_ref = refs
    e, f = pl.program_id(1), pl.program_id(2)

    @pl.when((e == 0) & (f == 0))
    def _():
        o_ref[...] = x1_ref[...]

    h = h_ref[...]
    act = _silu(_dot(h, wg_ref[...], hp)) * _dot(h, wu_ref[...], hp)
    y = _dot(act, wd_ref[...], hp)
    if use_comb:
        comb = comb_ref[...]
        lane = lax.broadcasted_iota(jnp.int32, comb.shape, 1)
        y = y * jnp.sum(jnp.where(lane == e, comb, 0.0), axis=-1, keepdims=True)
    o_ref[...] += y


def _ffn(x1, h, comb, w_gu, w_down, *, hp, tm, tf):
    n, d = x1.shape
    n_e, f_dim = w_down.shape[0], w_down.shape[1]
    assert n % tm == 0 and f_dim % tf == 0
    nf = f_dim // tf
    row = lambda w: pl.BlockSpec((tm, w), lambda i, e, f: (i, 0))
    in_specs = [row(d), row(d)]
    args = [x1, h]
    if comb is not None:
        in_specs.append(row(LANES))
        args.append(comb)
    in_specs += [pl.BlockSpec((None, d, tf), lambda i, e, f: (e, 0, f)),
                 pl.BlockSpec((None, d, tf), lambda i, e, f: (e, 0, f + nf)),
                 pl.BlockSpec((None, tf, d), lambda i, e, f: (e, f, 0))]
    args += [w_gu, w_gu, w_down]
    return pl.pallas_call(
        functools.partial(_ffn_kernel, hp=hp, use_comb=comb is not None),
        grid=(n // tm, n_e, nf), in_specs=in_specs, out_specs=row(d),
        out_shape=jax.ShapeDtypeStruct((n, d), F32),
        compiler_params=_cparams(("parallel", "arbitrary", "arbitrary")), name="ffn",
    )(*args)


def _split_w_in(w):
    o = [0]
    for s in (DN_CH, DN_QK, DN_H, DN_H, DA_W, DA_W, DA_W):
        o.append(o[-1] + s)
    ba = jnp.pad(w[:, o[2]:o[4]], ((0, 0), (0, LANES - 2 * DN_H)))
    return [w[:, o[0]:o[1]], w[:, o[1]:o[2]], ba, w[:, o[4]:o[5]], w[:, o[5]:o[6]], w[:, o[6]:o[7]], w[:, o[7]:]]


def _trunk(x, pos0, conv0, ssm0, paged, p, *, hp):
    b, t, d = x.shape
    n = b * t
    depth = p['w_in'].shape[0]
    wdt = F32 if hp else BF16
    cast = lambda w: w.astype(wdt)
    x2 = x.reshape(n, d)
    tm = min(256, n)
    cos, sin = _rope_tables(pos0 + jnp.arange(t, dtype=jnp.int32))
    chunk = DN_CHUNK if t >= DN_CHUNK else -(-t // 16) * 16
    t_pad = -(-t // chunk) * chunk
    tt = min(128, t_pad)
    k_rows, v_rows, ssm_out, conv_out = [], [], [], []
    kv_rows = None
    for l in range(depth):
        lam_init = 0.8 - 0.6 * math.exp(-0.3 * l)
        ws = [cast(w) for w in _split_w_in(p['w_in'][l])]
        lams = (p['lambda_q1'][l], p['lambda_k1'][l], p['lambda_q2'][l], p['lambda_k2'][l])
        if paged is None:
            qkv_pre, z, ba, g_raw, q_s, k_bf, v_bf, k_t, v4 = _in_proj(
                x2, p['attn_norm'][l], ws, p['q_norm'][l], p['k_norm'][l], cos, sin, kv_rows,
                batch=b, tm=tm, layer=l, depth=depth)
            kv_rows = (k_t, v4)
            b_out = _flash(q_s, k_bf, v_bf, lams, p['subln'][l], batch=b, blk=min(512, t), lam_init=lam_init)
        else:
            outs = []
            for grp in (ws[0:3], ws[3:6], ws[6:7]):
                outs += _norm_proj(x2, p['attn_norm'][l], grp, hp=hp, tm=tm)
            qkv_pre, z, ba, q_raw, k_raw, v_raw, g_raw = outs
            q_s, k_rot, _, _ = _qk_prep(q_raw, k_raw, v_raw, p['q_norm'][l], p['k_norm'][l], cos, sin,
                                        hp=hp, tm=tm, q_dtype=F32)
            kpool, vpool, page_table = paged
            b_out = _decode_attn(q_s, k_rot, v_raw, kpool, vpool, page_table, l, lams, p['subln'][l],
                                 lam_init=lam_init, group=math.gcd(32, page_table.shape[1]))
            k_rows.append(k_rot.reshape(b, t, 2 * DA_H, DA_DH))
            v_rows.append(v_raw.reshape(b, t, DA_H, 2 * DA_DH))

        def pad_t(a):
            if t_pad == t:
                return a
            return jnp.pad(a.reshape(b, t, -1), ((0, 0), (0, t_pad - t), (0, 0))).reshape(b * t_pad, -1)

        a_out, s_new = _deltanet(pad_t(qkv_pre), pad_t(z), pad_t(ba), conv0[l], ssm0[l], p['conv_w'][l],
                                 p['a_log'][l], p['dt_bias'][l], p['dn_norm'][l],
                                 batch=b, t_pad=t_pad, t_real=t, tt=tt, hp=hp)
        if t_pad != t:
            a_out = a_out.reshape(b, t_pad, -1)[:, :t].reshape(n, -1)

        moe = l % 2 == 1
        mixed = _mix(x2, a_out, b_out, g_raw, cast(p['w_a_proj'][l]), cast(p['w_b_proj'][l]), cast(p['w_out'][l]),
                     p['ffn_norm'][l], p['w_router'][l // 2] if moe else None, hp=hp, tm=min(512, n), h_dtype=wdt,
                     gated=paged is None)
        if moe:
            x1, h2, comb = mixed
            x2 = _ffn(x1, h2, comb, cast(p['w_exp_gu'][l // 2]), cast(p['w_exp_down'][l // 2]),
                      hp=hp, tm=min(1024, n), tf=p['w_exp_down'].shape[2])
        else:
            x1, h2 = mixed
            x2 = _ffn(x1, h2, None, cast(p['w_ffn_gu'][l // 2][None]), cast(p['w_ffn_down'][l // 2][None]),
                      hp=hp, tm=min(1024, n), tf=p['w_ffn_down'].shape[1] // 2)

        ssm_out.append(s_new)
        tail = qkv_pre.reshape(b, t, DN_CH)[:, max(t - (CONV_W - 1), 0):]
        conv_out.append(jnp.concatenate([conv0[l], tail], axis=1)[:, -(CONV_W - 1):])
    if paged is None:
        k_t, v4 = kv_rows
        k_all = jnp.transpose(k_t.reshape(depth, b, 2 * DA_H, DA_DH, t), (0, 1, 4, 2, 3))
        v_all = v4.reshape(depth, b, t, DA_H, 2 * DA_DH)
    else:
        k_all, v_all = jnp.stack(k_rows), jnp.stack(v_rows)
    return x2.reshape(b, t, d), k_all, v_all, jnp.stack(ssm_out), jnp.stack(conv_out)


def kernel(x_prompt, x_sample, cache_k, cache_v, state_ssm, state_conv, page_table, attn_norm, w_in, conv_w,
           a_log, dt_bias, dn_norm, q_norm, k_norm, lambda_q1, lambda_k1, lambda_q2, lambda_k2, subln,
           w_a_proj, w_b_proj, w_out, ffn_norm, w_ffn_gu, w_ffn_down, w_router, w_exp_gu, w_exp_down):
    p = dict(attn_norm=attn_norm, w_in=w_in, conv_w=conv_w, a_log=a_log, dt_bias=dt_bias, dn_norm=dn_norm,
             q_norm=q_norm, k_norm=k_norm, lambda_q1=lambda_q1, lambda_k1=lambda_k1, lambda_q2=lambda_q2,
             lambda_k2=lambda_k2, subln=subln, w_a_proj=w_a_proj, w_b_proj=w_b_proj, w_out=w_out,
             ffn_norm=ffn_norm, w_ffn_gu=w_ffn_gu, w_ffn_down=w_ffn_down, w_router=w_router,
             w_exp_gu=w_exp_gu, w_exp_down=w_exp_down)
    depth = w_in.shape[0]
    bp = x_prompt.shape[0]
    assert x_sample.shape[1] == 1, "the decode path handles one new token per sequence"
    assert cache_k.shape[2] == PAGE
    conv0 = jnp.zeros((depth, bp, CONV_W - 1, DN_CH), F32)
    ssm0 = jnp.zeros((depth, bp, DN_H, DN_DK, DN_DV), F32)
    y_p, k_p, v_p, ssm_p, conv_p = _trunk(x_prompt, 0, conv0, ssm0, None, p, hp=False)

    past_len = page_table.shape[1] * PAGE
    n_pool = cache_k.shape[1]
    kpool = jnp.transpose(cache_k, (0, 1, 3, 4, 2))
    vpool = cache_v.reshape(depth, n_pool, PAGE * DA_H, 2 * DA_DH)
    y_s, k_s, v_s, ssm_s, conv_s = _trunk(x_sample, past_len, state_conv, state_ssm, (kpool, vpool, page_table), p,
                                          hp=True)
    return (y_p, y_s, k_p, v_p, ssm_p, conv_p, k_s, v_s, ssm_s, conv_s)
```

```python
import functools
import math

import jax
import jax.numpy as jnp
from jax import lax
from jax.experimental import pallas as pl
from jax.experimental.pallas import tpu as pltpu

F32 = jnp.float32
BF16 = jnp.bfloat16
EPS = 1e-6
LANES = 128
VMEM_LIMIT = 56 * 1024 * 1024

DN_H, DN_DK, DN_DV = 4, 128, 128
DN_QK = DN_H * DN_DK
DN_CH = 3 * DN_QK
CONV_W = 4
DN_CHUNK = 64
DA_H, DA_DH = 4, 64
DA_W = 2 * DA_H * DA_DH
ROPE_THETA = 10000.0
PAGE = 128

_NN = (((1,), (0,)), ((), ()))
_NT = (((1,), (1,)), ((), ()))
_BNN = (((2,), (1,)), ((0,), (0,)))
_BNT = (((2,), (2,)), ((0,), (0,)))
assert 2 * DA_DH == LANES and DN_DV == LANES


def _cparams(sem):
    return pltpu.CompilerParams(dimension_semantics=sem, vmem_limit_bytes=VMEM_LIMIT)


def _mm(a, b, dims=_NN):
    return lax.dot_general(a, b, dims, preferred_element_type=F32)


def _split(a):
    if a.dtype == BF16:
        return a, None
    hi = a.astype(BF16)
    lo = (a - hi.astype(F32)).astype(BF16)
    return hi, lo


def _dot(a, b, hp, dims=_NN):
    if not hp:
        return _mm(a.astype(BF16), b.astype(BF16), dims)
    ah, al = _split(a)
    bh, bl = _split(b)
    out = _mm(ah, bh, dims)
    if al is not None:
        out = out + _mm(al, bh, dims)
    if bl is not None:
        out = out + _mm(ah, bl, dims)
    return out


def _split3(a):
    hi = a.astype(BF16)
    r = a - hi.astype(F32)
    mid = r.astype(BF16)
    lo = (r - mid.astype(F32)).astype(BF16)
    return hi, mid, lo


def _dot_exact_lhs(a_bf16, b, dims=_NN):
    hi, mid, lo = _split3(b)
    return _mm(a_bf16, hi, dims) + (_mm(a_bf16, mid, dims) + _mm(a_bf16, lo, dims))


def _dot_exact_rhs(a, b_bf16):
    hi, mid, lo = _split3(a)
    return _mm(hi, b_bf16) + (_mm(mid, b_bf16) + _mm(lo, b_bf16))


def _rms(x, g):
    return x * lax.rsqrt(jnp.mean(x * x, axis=-1, keepdims=True) + EPS) * g


def _silu(x):
    return x * jax.nn.sigmoid(x)


def _norm_proj_kernel(x_ref, g_ref, *refs, n_w, hp):
    w_refs, o_refs = refs[:n_w], refs[n_w:]
    h = _rms(x_ref[...], g_ref[...])
    if not hp:
        h = h.astype(BF16)
    for w_ref, o_ref in zip(w_refs, o_refs):
        o_ref[...] = _dot(h, w_ref[...], hp).astype(o_ref.dtype)


def _norm_proj(x, g, ws, *, hp, tm):
    n, d = x.shape
    assert n % tm == 0
    in_specs = [pl.BlockSpec((tm, d), lambda i: (i, 0)), pl.BlockSpec((1, d), lambda i: (0, 0))]
    in_specs += [pl.BlockSpec(w.shape, lambda i: (0, 0)) for w in ws]
    out_specs = [pl.BlockSpec((tm, w.shape[1]), lambda i: (i, 0)) for w in ws]
    out_shape = [jax.ShapeDtypeStruct((n, w.shape[1]), F32) for w in ws]
    return pl.pallas_call(
        functools.partial(_norm_proj_kernel, n_w=len(ws), hp=hp),
        grid=(n // tm,), in_specs=in_specs, out_specs=out_specs, out_shape=out_shape,
        compiler_params=_cparams(("parallel",)), name="norm_proj",
    )(x, g.reshape(1, d), *ws)


def _qk_prep_kernel(q_ref, k_ref, v_ref, qn_ref, kn_ref, cos_ref, sin_ref, bd_ref,
                    qo_ref, ko_ref, kb_ref, vb_ref, *, hp, q_scale):
    tm = q_ref.shape[0]
    lane = lax.broadcasted_iota(jnp.int32, (tm, DA_W), 1)
    first_half = (lane % DA_DH) < (DA_DH // 2)
    bd = bd_ref[...]
    cos, sin = cos_ref[...], sin_ref[...]
    q = _norm_rope(q_ref[...], qn_ref[...], bd, cos, sin, first_half, hp) * q_scale
    k = _norm_rope(k_ref[...], kn_ref[...], bd, cos, sin, first_half, hp)
    qo_ref[...] = q.astype(qo_ref.dtype)
    ko_ref[...] = k
    kb_ref[...] = k.astype(BF16)
    vb_ref[...] = v_ref[...].astype(BF16)


def _rope_tables(pos):
    half = DA_DH // 2
    inv = ROPE_THETA ** (-jnp.arange(half, dtype=F32) / half)
    ang = pos.astype(F32)[:, None] * inv[None, :]
    cos, sin = jnp.cos(ang), jnp.sin(ang)
    cos = jnp.tile(jnp.concatenate([cos, cos], axis=-1), (1, 2 * DA_H))
    sin = jnp.tile(jnp.concatenate([-sin, sin], axis=-1), (1, 2 * DA_H))
    return cos, sin


def _qk_prep(q_raw, k_raw, v_raw, qn, kn, cos, sin, *, hp, tm, q_dtype):
    n = q_raw.shape[0]
    t = cos.shape[0]
    if t % tm:
        cos, sin = jnp.tile(cos, (n // t, 1)), jnp.tile(sin, (n // t, 1))
        t = n
    assert n % tm == 0 and t % tm == 0
    nt = t // tm
    idx = jnp.arange(DA_W) // DA_DH
    bd = (idx[:, None] == idx[None, :]).astype(BF16)
    row = pl.BlockSpec((tm, DA_W), lambda i: (i, 0))
    vec = pl.BlockSpec((1, DA_W), lambda i: (0, 0))
    tab = pl.BlockSpec((tm, DA_W), lambda i: (i % nt, 0))
    return pl.pallas_call(
        functools.partial(_qk_prep_kernel, hp=hp, q_scale=DA_DH ** -0.5),
        grid=(n // tm,),
        in_specs=[row, row, row, vec, vec, tab, tab, pl.BlockSpec((DA_W, DA_W), lambda i: (0, 0))],
        out_specs=[row, row, row, row],
        out_shape=[jax.ShapeDtypeStruct((n, DA_W), q_dtype), jax.ShapeDtypeStruct((n, DA_W), F32),
                   jax.ShapeDtypeStruct((n, DA_W), BF16), jax.ShapeDtypeStruct((n, DA_W), BF16)],
        compiler_params=_cparams(("parallel",)), name="qk_prep",
    )(q_raw, k_raw, v_raw, jnp.tile(qn, 2 * DA_H).reshape(1, DA_W), jnp.tile(kn, 2 * DA_H).reshape(1, DA_W), cos, sin, bd)


def _norm_rope(x, g, bd, cos, sin, first_half, hp):
    sq = x * x
    ss = _dot_exact_rhs(sq, bd) if hp else _mm(sq.astype(BF16), bd)
    y = x * lax.rsqrt(ss * (1.0 / DA_DH) + EPS) * g
    partner = jnp.where(first_half, pltpu.roll(y, DA_W - DA_DH // 2, axis=1), pltpu.roll(y, DA_DH // 2, axis=1))
    return y * cos + partner * sin


def _in_proj_kernel(x_ref, g_ref, wqkv, wz, wba, wq, wk, wv, wg, qn_ref, kn_ref, cos_ref, sin_ref, bd_ref, *refs):
    qkv_o, z_o, ba_o, g_o, q_o, kb_o, vb_o, kt_o, v4_o = refs[-9:]
    tm = x_ref.shape[0]
    h = _rms(x_ref[...], g_ref[...]).astype(BF16)
    qkv_o[...] = _mm(h, wqkv[...])
    z_o[...] = _mm(h, wz[...])
    ba_o[...] = _mm(h, wba[...])
    g_o[...] = jax.nn.sigmoid(_mm(h, wg[...])).astype(BF16)
    lane = lax.broadcasted_iota(jnp.int32, (tm, DA_W), 1)
    first_half = (lane % DA_DH) < (DA_DH // 2)
    bd, cos, sin = bd_ref[...], cos_ref[...], sin_ref[...]
    q = _norm_rope(_mm(h, wq[...]), qn_ref[...], bd, cos, sin, first_half, False) * (DA_DH ** -0.5)
    k = _norm_rope(_mm(h, wk[...]), kn_ref[...], bd, cos, sin, first_half, False)
    v = _mm(h, wv[...])
    q_o[...] = q.astype(BF16)
    kb_o[...] = k.astype(BF16)
    vb_o[...] = v.astype(BF16)
    kt_o[...] = jnp.transpose(k)
    for hh in range(DA_H):
        v4_o[pl.ds(hh, tm, stride=DA_H), :] = v[:, 2 * DA_DH * hh:2 * DA_DH * (hh + 1)]


def _in_proj(x, g, ws, qn, kn, cos, sin, kv_rows, *, batch, tm, layer, depth):
    n, d = x.shape
    t = n // batch
    assert n % tm == 0 and t % tm == 0
    nt = t // tm
    idx = jnp.arange(DA_W) // DA_DH
    bd = (idx[:, None] == idx[None, :]).astype(BF16)
    row = lambda w, dt=F32: (pl.BlockSpec((tm, w), lambda i: (i, 0)), jax.ShapeDtypeStruct((n, w), dt))
    const = lambda shape: pl.BlockSpec(shape, lambda i: (0, 0))
    tab = pl.BlockSpec((tm, DA_W), lambda i: (i % nt, 0))
    outs = [row(DN_CH), row(DN_QK), row(LANES), row(ws[6].shape[1], BF16), row(DA_W, BF16), row(DA_W, BF16), row(DA_W, BF16),
            (pl.BlockSpec((None, None, DA_W, tm), lambda i: (layer, i // nt, 0, i % nt)),
             jax.ShapeDtypeStruct((depth, batch, DA_W, t), F32)),
            (pl.BlockSpec((None, tm * DA_H, 2 * DA_DH), lambda i: (layer, i, 0)),
             jax.ShapeDtypeStruct((depth, n * DA_H, 2 * DA_DH), F32))]
    in_specs = [pl.BlockSpec((tm, d), lambda i: (i, 0)), const((1, d))] + [const(w.shape) for w in ws]
    in_specs += [const((1, DA_W)), const((1, DA_W)), tab, tab, const((DA_W, DA_W))]
    args = [x, g.reshape(1, d), *ws, jnp.tile(qn, 2 * DA_H).reshape(1, DA_W), jnp.tile(kn, 2 * DA_H).reshape(1, DA_W),
            cos, sin, bd]
    aliases = {}
    if kv_rows is not None:
        aliases = {len(args): len(outs) - 2, len(args) + 1: len(outs) - 1}
        in_specs += [pl.BlockSpec(memory_space=pl.ANY)] * 2
        args += list(kv_rows)
    return pl.pallas_call(
        _in_proj_kernel, grid=(n // tm,), in_specs=in_specs,
        out_specs=[o[0] for o in outs], out_shape=[o[1] for o in outs], input_output_aliases=aliases,
        compiler_params=_cparams(("parallel",)), name="in_proj",
    )(*args)


def _lambda(lq1, lk1, lq2, lk2, lam_init):
    return (jnp.exp(jnp.sum(lq1[...] * lk1[...], axis=-1, keepdims=True))
            - jnp.exp(jnp.sum(lq2[...] * lk2[...], axis=-1, keepdims=True)) + lam_init)


def _flash_kernel(qi_ref, kj_ref, q_ref, k_ref, v_ref, lq1, lk1, lq2, lk2, sg_ref, o_ref, qs_scr, m_scr, l_scr,
                  acc_scr, s_scr, p_scr, al_scr, *, blk, lam_init):
    i, j = qi_ref[pl.program_id(1)], kj_ref[pl.program_id(1)]
    hw = 2 * DA_DH
    rb = min(64, blk)

    @pl.when(j == 0)
    def _():
        q = q_ref[...]
        lane = lax.broadcasted_iota(jnp.int32, (blk, hw), 1)
        zero = jnp.zeros((blk, hw), q.dtype)
        for h in range(DA_H):
            qh = q[:, hw * h:hw * (h + 1)]
            qs_scr[h, 0:blk, :] = jnp.where(lane < DA_DH, qh, zero)
            qs_scr[h, blk:2 * blk, :] = jnp.where(lane >= DA_DH, qh, zero)
        m_scr[...] = jnp.full(m_scr.shape, -jnp.inf, F32)
        l_scr[...] = jnp.zeros(l_scr.shape, F32)
        acc_scr[...] = jnp.zeros(acc_scr.shape, F32)

    def step(diag):
        if diag:
            r = lax.broadcasted_iota(jnp.int32, (2 * blk, blk), 0)
            c = lax.broadcasted_iota(jnp.int32, (2 * blk, blk), 1)
            keep = c <= jnp.where(r >= blk, r - blk, r)
        for h in range(DA_H):
            s = _mm(qs_scr[h], k_ref[:, hw * h:hw * (h + 1)], _NT)
            if diag:
                s = jnp.where(keep, s, -jnp.inf)
            s_scr[h] = s
            m_prev = m_scr[h]
            m_new = jnp.maximum(m_prev, jnp.max(s, axis=-1, keepdims=True))
            m_scr[h] = m_new
            al_scr[h] = jnp.exp(m_prev - m_new)
            for t in range(2 * blk // rb):
                rows = slice(t * rb, (t + 1) * rb)
                m_rows = m_scr[h, rows, :]
                p = jnp.exp(s_scr[h, rows, :] - jnp.concatenate([m_rows] * (blk // LANES), axis=1))
                l_scr[h, rows, :] = al_scr[h, rows, :] * l_scr[h, rows, :] + jnp.sum(p, axis=-1, keepdims=True)
                p_scr[h, rows, :] = p.astype(BF16)
            acc_scr[h] = al_scr[h] * acc_scr[h] + _mm(p_scr[h], v_ref[:, hw * h:hw * (h + 1)])

    @pl.when(j < i)
    def _():
        step(False)

    @pl.when(j == i)
    def _():
        step(True)
        lam = _lambda(lq1, lk1, lq2, lk2, lam_init)
        for h in range(DA_H):
            acc = acc_scr[h]
            l = l_scr[h]
            o = acc[0:blk] / l[0:blk] - lam * (acc[blk:] / l[blk:])
            o_ref[:, hw * h:hw * (h + 1)] = _rms(o, sg_ref[...]) * (1.0 - lam_init)


def _flash(q, k, v, lams, subln, *, batch, blk, lam_init):
    n = q.shape[0]
    nq = n // batch // blk
    hw = 2 * DA_DH
    pairs = [(i, j) for i in range(nq) for j in range(i + 1)]
    qi = jnp.array([ij[0] for ij in pairs], jnp.int32)
    kj = jnp.array([ij[1] for ij in pairs], jnp.int32)
    qspec = pl.BlockSpec((blk, DA_W), lambda b, s, qi, kj: (b * nq + qi[s], 0))
    kspec = pl.BlockSpec((blk, DA_W), lambda b, s, qi, kj: (b * nq + kj[s], 0))
    lspec = pl.BlockSpec((1, DA_DH), lambda b, s, qi, kj: (0, 0))
    return pl.pallas_call(
        functools.partial(_flash_kernel, blk=blk, lam_init=lam_init),
        grid_spec=pltpu.PrefetchScalarGridSpec(
            num_scalar_prefetch=2, grid=(batch, len(pairs)),
            in_specs=[qspec, kspec, kspec, lspec, lspec, lspec, lspec,
                      pl.BlockSpec((1, hw), lambda b, s, qi, kj: (0, 0))],
            out_specs=qspec,
            scratch_shapes=[pltpu.VMEM((DA_H, 2 * blk, hw), BF16), pltpu.VMEM((DA_H, 2 * blk, LANES), F32),
                            pltpu.VMEM((DA_H, 2 * blk, LANES), F32), pltpu.VMEM((DA_H, 2 * blk, hw), F32),
                            pltpu.VMEM((DA_H, 2 * blk, blk), F32), pltpu.VMEM((DA_H, 2 * blk, blk), BF16),
                            pltpu.VMEM((DA_H, 2 * blk, LANES), F32)]),
        out_shape=jax.ShapeDtypeStruct((n, DA_W), F32),
        compiler_params=_cparams(("parallel", "arbitrary")), name="flash_diff_attn",
    )(qi, kj, q, k, v, *[x.reshape(1, DA_DH) for x in lams], subln.reshape(1, hw))


def _decode_kernel(pt_ref, qbd_ref, q8_ref, kn_ref, vn_ref, lq1, lk1, lq2, lk2, sg_ref, *refs,
                   group, lam_init):
    k_refs, v_refs = refs[:group], refs[group:2 * group]
    o_ref, m_scr, l_scr, acc_scr = refs[2 * group:]
    s_idx = pl.program_id(1)
    n_maps = 2 * DA_H

    @pl.when(s_idx == 0)
    def _():
        s0 = jnp.sum(q8_ref[...] * kn_ref[...], axis=-1, keepdims=True)
        m_scr[...] = jnp.broadcast_to(s0, (n_maps, LANES))
        l_scr[...] = jnp.ones((n_maps, LANES), F32)
        for h in range(DA_H):
            acc_scr[h] = jnp.broadcast_to(vn_ref[h:h + 1, :], (n_maps, 2 * DA_DH))

    q_hi, q_lo = _split(qbd_ref[...])
    a16 = jnp.concatenate([q_hi, q_lo], axis=0)

    s_list = []
    for g in range(group):
        kt = k_refs[g][...].reshape(n_maps * DA_DH, PAGE)
        k_hi, k_lo = _split(kt)
        r = _mm(a16, k_hi) + _mm(a16, k_lo)
        s_list.append(r[0:n_maps] + r[n_maps:])
    m_prev = m_scr[...][:, :1]
    m_cur = s_list[0].max(axis=-1, keepdims=True)
    for s in s_list[1:]:
        m_cur = jnp.maximum(m_cur, s.max(axis=-1, keepdims=True))
    m_new = jnp.maximum(m_prev, m_cur)
    alpha = jnp.exp(m_prev - m_new)
    l_new = alpha * l_scr[...][:, :1]
    acc = [alpha * acc_scr[h] for h in range(DA_H)]
    for g in range(group):
        p = jnp.exp(s_list[g] - m_new)
        l_new = l_new + jnp.sum(p, axis=-1, keepdims=True)
        p_hi, p_lo = _split(p)
        p16 = jnp.concatenate([p_hi, p_lo], axis=0)
        for h in range(DA_H):
            v_hi, v_lo = _split(v_refs[g][pl.ds(h, PAGE, stride=DA_H), :])
            r = _mm(p16, v_hi) + _mm(p16, v_lo)
            acc[h] = acc[h] + (r[0:n_maps] + r[n_maps:])
    m_scr[...] = jnp.broadcast_to(m_new, (n_maps, LANES))
    l_scr[...] = jnp.broadcast_to(l_new, (n_maps, LANES))
    for h in range(DA_H):
        acc_scr[h] = acc[h]

    @pl.when(s_idx == pl.num_programs(1) - 1)
    def _():
        lam = _lambda(lq1, lk1, lq2, lk2, lam_init)
        for h in range(DA_H):
            o1 = acc[h][2 * h:2 * h + 1] / l_new[2 * h:2 * h + 1]
            o2 = acc[h][2 * h + 1:2 * h + 2] / l_new[2 * h + 1:2 * h + 2]
            o_ref[h:h + 1, :] = _rms(o1 - lam * o2, sg_ref[...]) * (1.0 - lam_init)


def _decode_attn(q, k_new, v_new, kpool, vpool, page_table, layer, lams, subln, *, lam_init, group):
    s, n_pages = page_table.shape
    assert n_pages % group == 0
    hw = 2 * DA_DH
    n_maps = 2 * DA_H
    q8 = q.reshape(s, n_maps, DA_DH)
    qbd = (q8[:, :, None, :] * jnp.eye(n_maps, dtype=F32)[None, :, :, None]).reshape(s, n_maps, DA_W)
    kn8 = k_new.reshape(s, n_maps, DA_DH)
    vn4 = v_new.reshape(s, DA_H, hw)

    small = lambda shape: pl.BlockSpec((None,) + shape, lambda b, t, pt: (b, 0, 0))
    lspec = pl.BlockSpec((1, DA_DH), lambda b, t, pt: (0, 0))
    in_specs = [small((n_maps, DA_W)), small((n_maps, DA_DH)), small((n_maps, DA_DH)), small((DA_H, hw)),
                lspec, lspec, lspec, lspec, pl.BlockSpec((1, hw), lambda b, t, pt: (0, 0))]
    in_specs += [pl.BlockSpec((None, None, n_maps, DA_DH, PAGE),
                              lambda b, t, pt, g=g: (layer, pt[b, t * group + g], 0, 0, 0)) for g in range(group)]
    in_specs += [pl.BlockSpec((None, None, PAGE * DA_H, hw),
                              lambda b, t, pt, g=g: (layer, pt[b, t * group + g], 0, 0)) for g in range(group)]
    out = pl.pallas_call(
        functools.partial(_decode_kernel, group=group, lam_init=lam_init),
        grid_spec=pltpu.PrefetchScalarGridSpec(
            num_scalar_prefetch=1, grid=(s, n_pages // group), in_specs=in_specs,
            out_specs=small((DA_H, hw)),
            scratch_shapes=[pltpu.VMEM((n_maps, LANES), F32), pltpu.VMEM((n_maps, LANES), F32),
                            pltpu.VMEM((DA_H, n_maps, hw), F32)]),
        out_shape=jax.ShapeDtypeStruct((s, DA_H, hw), F32),
        compiler_params=_cparams(("parallel", "arbitrary")), name="decode_diff_attn",
    )(page_table, qbd, q8, kn8, vn4, *[x.reshape(1, DA_DH) for x in lams], subln.reshape(1, hw),
      *([kpool] * group), *([vpool] * group))
    return out.reshape(s, DA_W)


def _tri_inverse(low, eye):
    a = -low
    t = eye + a
    ph, pl_ = _split(a)
    steps = int(math.log2(low.shape[-1])) - 1
    for _ in range(steps):
        p = _mm(ph, ph, _BNN) + (_mm(pl_, ph, _BNN) + _mm(ph, pl_, _BNN))
        ph, pl_ = _split(p)
        th, tl = _split(t)
        t = t + (_mm(th, ph, _BNN) + (_mm(tl, ph, _BNN) + _mm(th, pl_, _BNN)))
    return t


def _deltanet_kernel(qkv_ref, z_ref, ba_ref, conv0_ref, s0_ref, cw_ref, alog_ref, dtb_ref, ng_ref,
                     o_ref, s_out_ref, xbuf, s_scr, *, tt, t_real, hp):
    t = pl.program_id(1)
    ns = qkv_ref.shape[0]
    rows_all = ns * tt
    c = min(DN_CHUNK, tt)

    @pl.when(t == 0)
    def _():
        xbuf[:, 0:8, :] = conv0_ref[...]
        s_scr[...] = s0_ref[...].reshape(s_scr.shape)

    @pl.when(t > 0)
    def _():
        xbuf[:, 0:8, :] = xbuf[:, tt:tt + 8, :]

    xbuf[:, 8:8 + tt, :] = qkv_ref[...]
    cw = cw_ref[...]
    acts = []
    for u in range(ns):
        acc = xbuf[u, pl.ds(8 - (CONV_W - 1), tt), :] * cw[0:1]
        for jj in range(1, CONV_W):
            acc = acc + xbuf[u, pl.ds(8 - (CONV_W - 1) + jj, tt), :] * cw[jj:jj + 1]
        acts.append(_silu(acc))
    act = jnp.concatenate(acts, axis=0)

    ba = ba_ref[...].reshape(rows_all, LANES)
    beta_all = jax.nn.sigmoid(ba)
    sp_in = ba + dtb_ref[...]
    softplus = jnp.maximum(sp_in, 0.0) + jnp.log(1.0 + jnp.exp(-jnp.abs(sp_in)))
    g_all = -jnp.exp(alog_ref[...]) * softplus
    if t_real is not None:
        ridx = t * tt + lax.rem(lax.broadcasted_iota(jnp.int32, (rows_all, 1), 0), tt)
        live = ridx < t_real
        act = jnp.where(live, act, 0.0)
        beta_all = jnp.where(live, beta_all, 0.0)
        g_all = jnp.where(live, g_all, 0.0)

    ncs = tt // c
    nc = ns * ncs
    nb = nc * DN_H
    ri = lax.broadcasted_iota(jnp.int32, (nb, c, c), 1)
    ci = lax.broadcasted_iota(jnp.int32, (nb, c, c), 2)
    causal, strict = ri >= ci, ri > ci
    eye = jnp.where(ri == ci, 1.0, 0.0).astype(F32)

    def stack(arr, off, width):
        return jnp.stack([arr[cc * c:(cc + 1) * c, off + width * h:off + width * (h + 1)]
                          for cc in range(nc) for h in range(DN_H)], axis=0)

    rt = lax.broadcasted_iota(jnp.int32, (rows_all, rows_all), 0)
    ct = lax.broadcasted_iota(jnp.int32, (rows_all, rows_all), 1)
    chunk_tri = jnp.where((rt >= ct) & (rt // c == ct // c), 1.0, 0.0).astype(BF16)
    gc_all = _dot_exact_lhs(chunk_tri, g_all)

    q = stack(act, 0, DN_DK)
    k = stack(act, DN_QK, DN_DK)
    v = stack(act, 2 * DN_QK, DN_DV)
    q = q * lax.rsqrt(jnp.sum(q * q, axis=-1, keepdims=True) + EPS) * (DN_DK ** -0.5)
    k = k * lax.rsqrt(jnp.sum(k * k, axis=-1, keepdims=True) + EPS)
    beta = stack(beta_all, 0, 1)
    gc = stack(gc_all, DN_H, 1)
    gc_t = [jnp.transpose(gc_all[cc * c:(cc + 1) * c, :]) for cc in range(nc)]
    gc_row = jnp.stack([gc_t[cc][DN_H + h:DN_H + h + 1, :] for cc in range(nc) for h in range(DN_H)], axis=0)
    decay = jnp.where(causal, jnp.exp(jnp.where(causal, gc - gc_row, 0.0)), 0.0)
    kb, vb = k * beta, v * beta
    low = jnp.where(strict, _dot(kb, k, hp, _BNT) * decay, 0.0)
    tinv = _tri_inverse(low, eye)
    eg = jnp.exp(gc)
    sol = _dot(tinv, jnp.concatenate([vb, kb * eg], axis=-1), True, _BNN)
    u, w = sol[:, :, :DN_DV], sol[:, :, DN_DV:]
    qk = _dot(q, k, hp, _BNT) * decay
    gc_last = gc[:, c - 1:c, :]
    qd = q * eg
    kd = k * jnp.exp(gc_last - gc)
    gl = jnp.exp(gc_last)
    ng = ng_ref[...]

    s = s_scr[...]
    for cc in range(ncs):
        idx = [(sq * ncs + cc) * DN_H + h for sq in range(ns) for h in range(DN_H)]
        pick = lambda x: jnp.concatenate([x[(sq * ncs + cc) * DN_H:(sq * ncs + cc + 1) * DN_H] for sq in range(ns)], axis=0)
        v_new = pick(u) - _dot(pick(w), s, hp, _BNN)
        o = _dot(pick(qd), s, hp, _BNN) + _dot(pick(qk), v_new, hp, _BNN)
        kdt = jnp.stack([jnp.transpose(kd[i]) for i in idx], axis=0)
        s = s * pick(gl) + _dot(kdt, v_new, hp, _BNN)
        for sq in range(ns):
            for h in range(DN_H):
                zz = z_ref[sq, cc * c:(cc + 1) * c, DN_DV * h:DN_DV * (h + 1)]
                o_ref[sq, cc * c:(cc + 1) * c, DN_DV * h:DN_DV * (h + 1)] = _rms(o[sq * DN_H + h], ng) * _silu(zz)
    s_scr[...] = s

    @pl.when(t == pl.num_programs(1) - 1)
    def _():
        s_out_ref[...] = s_scr[...].reshape(s_out_ref.shape)


def _deltanet(qkv, z, ba, conv0, s0, conv_w, a_log, dt_bias, norm_g, *, batch, t_pad, t_real, tt, hp):
    n = qkv.shape[0]
    assert n == batch * t_pad and t_pad % tt == 0 and tt % min(DN_CHUNK, tt) == 0 and tt % 16 == 0
    nt = t_pad // tt
    conv0p = jnp.pad(conv0, ((0, 0), (8 - (CONV_W - 1), 0), (0, 0)))
    cwp = jnp.pad(conv_w, ((0, 8 - CONV_W), (0, 0)))
    lane_vec = lambda a: jnp.pad(a, (DN_H, LANES - 2 * DN_H)).reshape(1, LANES)
    ns = math.gcd(batch, 4)
    seq3 = lambda a: a.reshape(batch, t_pad, a.shape[-1])
    row = lambda w: pl.BlockSpec((ns, tt, w), lambda b, t: (b, t, 0))
    const = lambda shape: pl.BlockSpec(shape, lambda b, t: (0,) * len(shape))
    state = pl.BlockSpec((ns, DN_H, DN_DK, DN_DV), lambda b, t: (b, 0, 0, 0))
    a_out, s_out = pl.pallas_call(
        functools.partial(_deltanet_kernel, tt=tt, t_real=None if t_real == t_pad else t_real, hp=hp),
        grid=(batch // ns, nt),
        in_specs=[row(DN_CH), row(DN_QK), row(LANES),
                  pl.BlockSpec((ns, 8, DN_CH), lambda b, t: (b, 0, 0)), state,
                  const((8, DN_CH)), const((1, LANES)), const((1, LANES)), const((1, DN_DV))],
        out_specs=[row(DN_QK), state],
        out_shape=[jax.ShapeDtypeStruct((batch, t_pad, DN_QK), F32), jax.ShapeDtypeStruct(s0.shape, F32)],
        scratch_shapes=[pltpu.VMEM((ns, tt + 8, DN_CH), F32), pltpu.VMEM((ns * DN_H, DN_DK, DN_DV), F32)],
        compiler_params=_cparams(("parallel", "arbitrary")), name="deltanet",
    )(seq3(qkv), seq3(z), seq3(ba), conv0p, s0, cwp, lane_vec(a_log), lane_vec(dt_bias), norm_g.reshape(1, DN_DV))
    return a_out.reshape(n, DN_QK), s_out


def _mix_kernel(x_ref, a_ref, b_ref, ga_ref, gb_ref, wa_ref, wb_ref, wo_ref, fn_ref, *refs, hp, n_experts, gated):
    if n_experts:
        wr_ref, x1_ref, h_ref, comb_ref = refs
    else:
        x1_ref, h_ref = refs
    a = _dot(a_ref[...], wa_ref[...], hp)
    b = _dot(b_ref[...], wb_ref[...], hp)
    gate = (lambda r: r[...].astype(F32)) if gated else (lambda r: jax.nn.sigmoid(r[...]))
    mixed = gate(ga_ref) * a + gate(gb_ref) * b
    x1 = x_ref[...] + _dot(mixed, wo_ref[...], hp)
    x1_ref[...] = x1
    h = _rms(x1, fn_ref[...])
    h_ref[...] = h.astype(h_ref.dtype)
    if n_experts:
        logits = _dot(h, wr_ref[...], hp)
        lane = lax.broadcasted_iota(jnp.int32, logits.shape, 1).astype(F32)
        lg = jnp.where(lane < n_experts, logits, -jnp.inf)
        m1 = jnp.max(lg, axis=-1, keepdims=True)
        i1 = jnp.min(jnp.where(lg == m1, lane, float(LANES)), axis=-1, keepdims=True)
        lg2 = jnp.where(lane == i1, -jnp.inf, lg)
        m2 = jnp.max(lg2, axis=-1, keepdims=True)
        i2 = jnp.min(jnp.where(lg2 == m2, lane, float(LANES)), axis=-1, keepdims=True)
        e2 = jnp.exp(m2 - m1)
        den = 1.0 + e2
        comb_ref[...] = jnp.where(lane == i1, 1.0 / den, 0.0) + jnp.where(lane == i2, e2 / den, 0.0)


def _mix(x, a_out, b_out, g_raw, wa, wb, wo, fn, w_router, *, hp, tm, h_dtype, gated):
    n, d = x.shape
    assert n % tm == 0
    n_experts = 0 if w_router is None else w_router.shape[1]
    row = lambda w, c=0: pl.BlockSpec((tm, w), lambda i, c=c: (i, c))
    const = lambda shape: pl.BlockSpec(shape, lambda i: (0, 0))
    in_specs = [row(d), row(a_out.shape[1]), row(b_out.shape[1]), row(d, 0), row(d, 1),
                const(wa.shape), const(wb.shape), const(wo.shape), const((1, d))]
    args = [x, a_out, b_out, g_raw, g_raw, wa, wb, wo, fn.reshape(1, d)]
    out_specs = [row(d), row(d)]
    out_shape = [jax.ShapeDtypeStruct((n, d), F32), jax.ShapeDtypeStruct((n, d), h_dtype)]
    if n_experts:
        in_specs.append(const((d, LANES)))
        args.append(jnp.pad(w_router, ((0, 0), (0, LANES - n_experts))))
        out_specs.append(row(LANES))
        out_shape.append(jax.ShapeDtypeStruct((n, LANES), F32))
    return pl.pallas_call(
        functools.partial(_mix_kernel, hp=hp, n_experts=n_experts, gated=gated),
        grid=(n // tm,), in_specs=in_specs, out_specs=out_specs, out_shape=out_shape,
        compiler_params=_cparams(("parallel",)), name="mix_out",
    )(*args)


def _ffn_kernel(x1_ref, h_ref, *refs, hp, use_comb):
    if use_comb:
        comb_ref, wg_ref, wu_ref, wd_ref, o_ref = refs
    else:
        wg_ref, wu_ref, wd_ref, o_ref = refs
    e, f = pl.program_id(1), pl.program_id(2)

    @pl.when((e == 0) & (f == 0))
    def _():
        o_ref[...] = x1_ref[...]

    h = h_ref[...]
    act = _silu(_dot(h, wg_ref[...], hp)) * _dot(h, wu_ref[...], hp)
    y = _dot(act, wd_ref[...], hp)
    if use_comb:
        comb = comb_ref[...]
        lane = lax.broadcasted_iota(jnp.int32, comb.shape, 1)
        y = y * jnp.sum(jnp.where(lane == e, comb, 0.0), axis=-1, keepdims=True)
    o_ref[...] += y


def _ffn(x1, h, comb, w_gu, w_down, *, hp, tm, tf):
    n, d = x1.shape
    n_e, f_dim = w_down.shape[0], w_down.shape[1]
    assert n % tm == 0 and f_dim % tf == 0
    nf = f_dim // tf
    row = lambda w: pl.BlockSpec((tm, w), lambda i, e, f: (i, 0))
    in_specs = [row(d), row(d)]
    args = [x1, h]
    if comb is not None:
        in_specs.append(row(LANES))
        args.append(comb)
    in_specs += [pl.BlockSpec((None, d, tf), lambda i, e, f: (e, 0, f)),
                 pl.BlockSpec((None, d, tf), lambda i, e, f: (e, 0, f + nf)),
                 pl.BlockSpec((None, tf, d), lambda i, e, f: (e, f, 0))]
    args += [w_gu, w_gu, w_down]
    return pl.pallas_call(
        functools.partial(_ffn_kernel, hp=hp, use_comb=comb is not None),
        grid=(n // tm, n_e, nf), in_specs=in_specs, out_specs=row(d),
        out_shape=jax.ShapeDtypeStruct((n, d), F32),
        compiler_params=_cparams(("parallel", "arbitrary", "arbitrary")), name="ffn",
    )(*args)


def _split_w_in(w):
    o = [0]
    for s in (DN_CH, DN_QK, DN_H, DN_H, DA_W, DA_W, DA_W):
        o.append(o[-1] + s)
    ba = jnp.pad(w[:, o[2]:o[4]], ((0, 0), (0, LANES - 2 * DN_H)))
    return [w[:, o[0]:o[1]], w[:, o[1]:o[2]], ba, w[:, o[4]:o[5]], w[:, o[5]:o[6]], w[:, o[6]:o[7]], w[:, o[7]:]]


def _trunk(x, pos0, conv0, ssm0, paged, p, *, hp):
    b, t, d = x.shape
    n = b * t
    depth = p['w_in'].shape[0]
    wdt = F32 if hp else BF16
    cast = lambda w: w.astype(wdt)
    x2 = x.reshape(n, d)
    tm = min(256, n)
    cos, sin = _rope_tables(pos0 + jnp.arange(t, dtype=jnp.int32))
    chunk = DN_CHUNK if t >= DN_CHUNK else -(-t // 16) * 16
    t_pad = -(-t // chunk) * chunk
    tt = min(128, t_pad)
    k_rows, v_rows, ssm_out, conv_out = [], [], [], []
    kv_rows = None
    for l in range(depth):
        lam_init = 0.8 - 0.6 * math.exp(-0.3 * l)
        ws = [cast(w) for w in _split_w_in(p['w_in'][l])]
        lams = (p['lambda_q1'][l], p['lambda_k1'][l], p['lambda_q2'][l], p['lambda_k2'][l])
        if paged is None:
            qkv_pre, z, ba, g_raw, q_s, k_bf, v_bf, k_t, v4 = _in_proj(
                x2, p['attn_norm'][l], ws, p['q_norm'][l], p['k_norm'][l], cos, sin, kv_rows,
                batch=b, tm=tm, layer=l, depth=depth)
            kv_rows = (k_t, v4)
            b_out = _flash(q_s, k_bf, v_bf, lams, p['subln'][l], batch=b, blk=min(512, t), lam_init=lam_init)
        else:
            outs = []
            for grp in (ws[0:3], ws[3:6], ws[6:7]):
                outs += _norm_proj(x2, p['attn_norm'][l], grp, hp=hp, tm=tm)
            qkv_pre, z, ba, q_raw, k_raw, v_raw, g_raw = outs
            q_s, k_rot, _, _ = _qk_prep(q_raw, k_raw, v_raw, p['q_norm'][l], p['k_norm'][l], cos, sin,
                                        hp=hp, tm=tm, q_dtype=F32)
            kpool, vpool, page_table = paged
            b_out = _decode_attn(q_s, k_rot, v_raw, kpool, vpool, page_table, l, lams, p['subln'][l],
                                 lam_init=lam_init, group=math.gcd(32, page_table.shape[1]))
            k_rows.append(k_rot.reshape(b, t, 2 * DA_H, DA_DH))
            v_rows.append(v_raw.reshape(b, t, DA_H, 2 * DA_DH))

        def pad_t(a):
            if t_pad == t:
                return a
            return jnp.pad(a.reshape(b, t, -1), ((0, 0), (0, t_pad - t), (0, 0))).reshape(b * t_pad, -1)

        a_out, s_new = _deltanet(pad_t(qkv_pre), pad_t(z), pad_t(ba), conv0[l], ssm0[l], p['conv_w'][l],
                                 p['a_log'][l], p['dt_bias'][l], p['dn_norm'][l],
                                 batch=b, t_pad=t_pad, t_real=t, tt=tt, hp=hp)
        if t_pad != t:
            a_out = a_out.reshape(b, t_pad, -1)[:, :t].reshape(n, -1)

        moe = l % 2 == 1
        mixed = _mix(x2, a_out, b_out, g_raw, cast(p['w_a_proj'][l]), cast(p['w_b_proj'][l]), cast(p['w_out'][l]),
                     p['ffn_norm'][l], p['w_router'][l // 2] if moe else None, hp=hp, tm=min(512, n), h_dtype=wdt,
                     gated=paged is None)
        if moe:
            x1, h2, comb = mixed
            x2 = _ffn(x1, h2, comb, cast(p['w_exp_gu'][l // 2]), cast(p['w_exp_down'][l // 2]),
                      hp=hp, tm=min(1024, n), tf=p['w_exp_down'].shape[2])
        else:
            x1, h2 = mixed
            x2 = _ffn(x1, h2, None, cast(p['w_ffn_gu'][l // 2][None]), cast(p['w_ffn_down'][l // 2][None]),
                      hp=hp, tm=min(1024, n), tf=p['w_ffn_down'].shape[1] // 2)

        ssm_out.append(s_new)
        tail = qkv_pre.reshape(b, t, DN_CH)[:, max(t - (CONV_W - 1), 0):]
        conv_out.append(jnp.concatenate([conv0[l], tail], axis=1)[:, -(CONV_W - 1):])
    if paged is None:
        k_t, v4 = kv_rows
        k_all = jnp.transpose(k_t.reshape(depth, b, 2 * DA_H, DA_DH, t), (0, 1, 4, 2, 3))
        v_all = v4.reshape(depth, b, t, DA_H, 2 * DA_DH)
    else:
        k_all, v_all = jnp.stack(k_rows), jnp.stack(v_rows)
    return x2.reshape(b, t, d), k_all, v_all, jnp.stack(ssm_out), jnp.stack(conv_out)


def kernel(x_prompt, x_sample, cache_k, cache_v, state_ssm, state_conv, page_table, attn_norm, w_in, conv_w,
           a_log, dt_bias, dn_norm, q_norm, k_norm, lambda_q1, lambda_k1, lambda_q2, lambda_k2, subln,
           w_a_proj, w_b_proj, w_out, ffn_norm, w_ffn_gu, w_ffn_down, w_router, w_exp_gu, w_exp_down):
    p = dict(attn_norm=attn_norm, w_in=w_in, conv_w=conv_w, a_log=a_log, dt_bias=dt_bias, dn_norm=dn_norm,
             q_norm=q_norm, k_norm=k_norm, lambda_q1=lambda_q1, lambda_k1=lambda_k1, lambda_q2=lambda_q2,
             lambda_k2=lambda_k2, subln=subln, w_a_proj=w_a_proj, w_b_proj=w_b_proj, w_out=w_out,
             ffn_norm=ffn_norm, w_ffn_gu=w_ffn_gu, w_ffn_down=w_ffn_down, w_router=w_router,
             w_exp_gu=w_exp_gu, w_exp_down=w_exp_down)
    depth = w_in.shape[0]
    bp = x_prompt.shape[0]
    assert x_sample.shape[1] == 1, "the decode path handles one new token per sequence"
    assert cache_k.shape[2] == PAGE
    conv0 = jnp.zeros((depth, bp, CONV_W - 1, DN_CH), F32)
    ssm0 = jnp.zeros((depth, bp, DN_H, DN_DK, DN_DV), F32)
    y_p, k_p, v_p, ssm_p, conv_p = _trunk(x_prompt, 0, conv0, ssm0, None, p, hp=False)

    past_len = page_table.shape[1] * PAGE
    n_pool = cache_k.shape[1]
    kpool = jnp.transpose(cache_k, (0, 1, 3, 4, 2))
    vpool = cache_v.reshape(depth, n_pool, PAGE * DA_H, 2 * DA_DH)
    y_s, k_s, v_s, ssm_s, conv_s = _trunk(x_sample, past_len, state_conv, state_ssm, (kpool, vpool, page_table), p,
                                          hp=True)
    return (y_p, y_s, k_p, v_p, ssm_p, conv_p, k_s, v_s, ssm_s, conv_s)
```

```python
import functools
import math

import jax
import jax.numpy as jnp
from jax import lax
from jax.experimental import pallas as pl
from jax.experimental.pallas import tpu as pltpu

F32 = jnp.float32
BF16 = jnp.bfloat16
EPS = 1e-6
LANES = 128
VMEM_LIMIT = 56 * 1024 * 1024

DN_H, DN_DK, DN_DV = 4, 128, 128
DN_QK = DN_H * DN_DK
DN_CH = 3 * DN_QK
CONV_W = 4
DN_CHUNK = 64
DA_H, DA_DH = 4, 64
DA_W = 2 * DA_H * DA_DH
ROPE_THETA = 10000.0
PAGE = 128

_NN = (((1,), (0,)), ((), ()))
_NT = (((1,), (1,)), ((), ()))
_BNN = (((2,), (1,)), ((0,), (0,)))
_BNT = (((2,), (2,)), ((0,), (0,)))
assert 2 * DA_DH == LANES and DN_DV == LANES


def _cparams(sem):
    return pltpu.CompilerParams(dimension_semantics=sem, vmem_limit_bytes=VMEM_LIMIT)


def _mm(a, b, dims=_NN):
    return lax.dot_general(a, b, dims, preferred_element_type=F32)


def _split(a):
    if a.dtype == BF16:
        return a, None
    hi = a.astype(BF16)
    lo = (a - hi.astype(F32)).astype(BF16)
    return hi, lo


def _dot(a, b, hp, dims=_NN):
    if not hp:
        return _mm(a.astype(BF16), b.astype(BF16), dims)
    ah, al = _split(a)
    bh, bl = _split(b)
    out = _mm(ah, bh, dims)
    if al is not None:
        out = out + _mm(al, bh, dims)
    if bl is not None:
        out = out + _mm(ah, bl, dims)
    return out


def _split3(a):
    hi = a.astype(BF16)
    r = a - hi.astype(F32)
    mid = r.astype(BF16)
    lo = (r - mid.astype(F32)).astype(BF16)
    return hi, mid, lo


def _dot_exact_lhs(a_bf16, b, dims=_NN):
    hi, mid, lo = _split3(b)
    return _mm(a_bf16, hi, dims) + (_mm(a_bf16, mid, dims) + _mm(a_bf16, lo, dims))


def _dot_exact_rhs(a, b_bf16):
    hi, mid, lo = _split3(a)
    return _mm(hi, b_bf16) + (_mm(mid, b_bf16) + _mm(lo, b_bf16))


def _rms(x, g):
    return x * lax.rsqrt(jnp.mean(x * x, axis=-1, keepdims=True) + EPS) * g


def _silu(x):
    return x * jax.nn.sigmoid(x)


def _norm_proj_kernel(x_ref, g_ref, *refs, n_w, hp):
    w_refs, o_refs = refs[:n_w], refs[n_w:]
    h = _rms(x_ref[...], g_ref[...])
    if not hp:
        h = h.astype(BF16)
    for w_ref, o_ref in zip(w_refs, o_refs):
        o_ref[...] = _dot(h, w_ref[...], hp).astype(o_ref.dtype)


def _norm_proj(x, g, ws, *, hp, tm):
    n, d = x.shape
    assert n % tm == 0
    in_specs = [pl.BlockSpec((tm, d), lambda i: (i, 0)), pl.BlockSpec((1, d), lambda i: (0, 0))]
    in_specs += [pl.BlockSpec(w.shape, lambda i: (0, 0)) for w in ws]
    out_specs = [pl.BlockSpec((tm, w.shape[1]), lambda i: (i, 0)) for w in ws]
    out_shape = [jax.ShapeDtypeStruct((n, w.shape[1]), F32) for w in ws]
    return pl.pallas_call(
        functools.partial(_norm_proj_kernel, n_w=len(ws), hp=hp),
        grid=(n // tm,), in_specs=in_specs, out_specs=out_specs, out_shape=out_shape,
        compiler_params=_cparams(("parallel",)), name="norm_proj",
    )(x, g.reshape(1, d), *ws)


def _qk_prep_kernel(q_ref, k_ref, v_ref, qn_ref, kn_ref, cos_ref, sin_ref, bd_ref,
                    qo_ref, ko_ref, kb_ref, vb_ref, *, hp, q_scale):
    tm = q_ref.shape[0]
    lane = lax.broadcasted_iota(jnp.int32, (tm, DA_W), 1)
    first_half = (lane % DA_DH) < (DA_DH // 2)
    bd = bd_ref[...]
    cos, sin = cos_ref[...], sin_ref[...]
    q = _norm_rope(q_ref[...], qn_ref[...], bd, cos, sin, first_half, hp) * q_scale
    k = _norm_rope(k_ref[...], kn_ref[...], bd, cos, sin, first_half, hp)
    qo_ref[...] = q.astype(qo_ref.dtype)
    ko_ref[...] = k
    kb_ref[...] = k.astype(BF16)
    vb_ref[...] = v_ref[...].astype(BF16)


def _rope_tables(pos):
    half = DA_DH // 2
    inv = ROPE_THETA ** (-jnp.arange(half, dtype=F32) / half)
    ang = pos.astype(F32)[:, None] * inv[None, :]
    cos, sin = jnp.cos(ang), jnp.sin(ang)
    cos = jnp.tile(jnp.concatenate([cos, cos], axis=-1), (1, 2 * DA_H))
    sin = jnp.tile(jnp.concatenate([-sin, sin], axis=-1), (1, 2 * DA_H))
    return cos, sin


def _qk_prep(q_raw, k_raw, v_raw, qn, kn, cos, sin, *, hp, tm, q_dtype):
    n = q_raw.shape[0]
    t = cos.shape[0]
    if t % tm:
        cos, sin = jnp.tile(cos, (n // t, 1)), jnp.tile(sin, (n // t, 1))
        t = n
    assert n % tm == 0 and t % tm == 0
    nt = t // tm
    idx = jnp.arange(DA_W) // DA_DH
    bd = (idx[:, None] == idx[None, :]).astype(BF16)
    row = pl.BlockSpec((tm, DA_W), lambda i: (i, 0))
    vec = pl.BlockSpec((1, DA_W), lambda i: (0, 0))
    tab = pl.BlockSpec((tm, DA_W), lambda i: (i % nt, 0))
    return pl.pallas_call(
        functools.partial(_qk_prep_kernel, hp=hp, q_scale=DA_DH ** -0.5),
        grid=(n // tm,),
        in_specs=[row, row, row, vec, vec, tab, tab, pl.BlockSpec((DA_W, DA_W), lambda i: (0, 0))],
        out_specs=[row, row, row, row],
        out_shape=[jax.ShapeDtypeStruct((n, DA_W), q_dtype), jax.ShapeDtypeStruct((n, DA_W), F32),
                   jax.ShapeDtypeStruct((n, DA_W), BF16), jax.ShapeDtypeStruct((n, DA_W), BF16)],
        compiler_params=_cparams(("parallel",)), name="qk_prep",
    )(q_raw, k_raw, v_raw, jnp.tile(qn, 2 * DA_H).reshape(1, DA_W), jnp.tile(kn, 2 * DA_H).reshape(1, DA_W), cos, sin, bd)


def _norm_rope(x, g, bd, cos, sin, first_half, hp):
    sq = x * x
    ss = _dot_exact_rhs(sq, bd) if hp else _mm(sq.astype(BF16), bd)
    y = x * lax.rsqrt(ss * (1.0 / DA_DH) + EPS) * g
    partner = jnp.where(first_half, pltpu.roll(y, DA_W - DA_DH // 2, axis=1), pltpu.roll(y, DA_DH // 2, axis=1))
    return y * cos + partner * sin


def _in_proj_kernel(x_ref, g_ref, wqkv, wz, wba, wq, wk, wv, wg, qn_ref, kn_ref, cos_ref, sin_ref, bd_ref, *refs):
    qkv_o, z_o, ba_o, g_o, q_o, kb_o, vb_o, kt_o, v4_o = refs[-9:]
    tm = x_ref.shape[0]
    h = _rms(x_ref[...], g_ref[...]).astype(BF16)
    qkv_o[...] = _mm(h, wqkv[...])
    z_o[...] = _mm(h, wz[...])
    ba_o[...] = _mm(h, wba[...])
    g_o[...] = jax.nn.sigmoid(_mm(h, wg[...])).astype(BF16)
    lane = lax.broadcasted_iota(jnp.int32, (tm, DA_W), 1)
    first_half = (lane % DA_DH) < (DA_DH // 2)
    bd, cos, sin = bd_ref[...], cos_ref[...], sin_ref[...]
    q = _norm_rope(_mm(h, wq[...]), qn_ref[...], bd, cos, sin, first_half, False) * (DA_DH ** -0.5)
    k = _norm_rope(_mm(h, wk[...]), kn_ref[...], bd, cos, sin, first_half, False)
    v = _mm(h, wv[...])
    q_o[...] = q.astype(BF16)
    kb_o[...] = k.astype(BF16)
    vb_o[...] = v.astype(BF16)
    kt_o[...] = jnp.transpose(k)
    for hh in range(DA_H):
        v4_o[pl.ds(hh, tm, stride=DA_H), :] = v[:, 2 * DA_DH * hh:2 * DA_DH * (hh + 1)]


def _in_proj(x, g, ws, qn, kn, cos, sin, kv_rows, *, batch, tm, layer, depth):
    n, d = x.shape
    t = n // batch
    assert n % tm == 0 and t % tm == 0
    nt = t // tm
    idx = jnp.arange(DA_W) // DA_DH
    bd = (idx[:, None] == idx[None, :]).astype(BF16)
    row = lambda w, dt=F32: (pl.BlockSpec((tm, w), lambda i: (i, 0)), jax.ShapeDtypeStruct((n, w), dt))
    const = lambda shape: pl.BlockSpec(shape, lambda i: (0, 0))
    tab = pl.BlockSpec((tm, DA_W), lambda i: (i % nt, 0))
    outs = [row(DN_CH), row(DN_QK), row(LANES), row(ws[6].shape[1], BF16), row(DA_W, BF16), row(DA_W, BF16), row(DA_W, BF16),
            (pl.BlockSpec((None, None, DA_W, tm), lambda i: (layer, i // nt, 0, i % nt)),
             jax.ShapeDtypeStruct((depth, batch, DA_W, t), F32)),
            (pl.BlockSpec((None, tm * DA_H, 2 * DA_DH), lambda i: (layer, i, 0)),
             jax.ShapeDtypeStruct((depth, n * DA_H, 2 * DA_DH), F32))]
    in_specs = [pl.BlockSpec((tm, d), lambda i: (i, 0)), const((1, d))] + [const(w.shape) for w in ws]
    in_specs += [const((1, DA_W)), const((1, DA_W)), tab, tab, const((DA_W, DA_W))]
    args = [x, g.reshape(1, d), *ws, jnp.tile(qn, 2 * DA_H).reshape(1, DA_W), jnp.tile(kn, 2 * DA_H).reshape(1, DA_W),
            cos, sin, bd]
    aliases = {}
    if kv_rows is not None:
        aliases = {len(args): len(outs) - 2, len(args) + 1: len(outs) - 1}
        in_specs += [pl.BlockSpec(memory_space=pl.ANY)] * 2
        args += list(kv_rows)
    return pl.pallas_call(
        _in_proj_kernel, grid=(n // tm,), in_specs=in_specs,
        out_specs=[o[0] for o in outs], out_shape=[o[1] for o in outs], input_output_aliases=aliases,
        compiler_params=_cparams(("parallel",)), name="in_proj",
    )(*args)


def _lambda(lq1, lk1, lq2, lk2, lam_init):
    return (jnp.exp(jnp.sum(lq1[...] * lk1[...], axis=-1, keepdims=True))
            - jnp.exp(jnp.sum(lq2[...] * lk2[...], axis=-1, keepdims=True)) + lam_init)


def _flash_kernel(qi_ref, kj_ref, q_ref, k_ref, v_ref, lq1, lk1, lq2, lk2, sg_ref, o_ref, qs_scr, m_scr, l_scr,
                  acc_scr, s_scr, p_scr, al_scr, *, blk, lam_init):
    i, j = qi_ref[pl.program_id(1)], kj_ref[pl.program_id(1)]
    hw = 2 * DA_DH
    rb = min(64, blk)

    @pl.when(j == 0)
    def _():
        q = q_ref[...]
        lane = lax.broadcasted_iota(jnp.int32, (blk, hw), 1)
        zero = jnp.zeros((blk, hw), q.dtype)
        for h in range(DA_H):
            qh = q[:, hw * h:hw * (h + 1)]
            qs_scr[h, 0:blk, :] = jnp.where(lane < DA_DH, qh, zero)
            qs_scr[h, blk:2 * blk, :] = jnp.where(lane >= DA_DH, qh, zero)
        m_scr[...] = jnp.full(m_scr.shape, -jnp.inf, F32)
        l_scr[...] = jnp.zeros(l_scr.shape, F32)
        acc_scr[...] = jnp.zeros(acc_scr.shape, F32)

    def step(diag):
        if diag:
            r = lax.broadcasted_iota(jnp.int32, (2 * blk, blk), 0)
            c = lax.broadcasted_iota(jnp.int32, (2 * blk, blk), 1)
            keep = c <= jnp.where(r >= blk, r - blk, r)
        for h in range(DA_H):
            s = _mm(qs_scr[h], k_ref[:, hw * h:hw * (h + 1)], _NT)
            if diag:
                s = jnp.where(keep, s, -jnp.inf)
            s_scr[h] = s
            m_prev = m_scr[h]
            m_new = jnp.maximum(m_prev, jnp.max(s, axis=-1, keepdims=True))
            m_scr[h] = m_new
            al_scr[h] = jnp.exp(m_prev - m_new)
            for t in range(2 * blk // rb):
                rows = slice(t * rb, (t + 1) * rb)
                m_rows = m_scr[h, rows, :]
                p = jnp.exp(s_scr[h, rows, :] - jnp.concatenate([m_rows] * (blk // LANES), axis=1))
                l_scr[h, rows, :] = al_scr[h, rows, :] * l_scr[h, rows, :] + jnp.sum(p, axis=-1, keepdims=True)
                p_scr[h, rows, :] = p.astype(BF16)
            acc_scr[h] = al_scr[h] * acc_scr[h] + _mm(p_scr[h], v_ref[:, hw * h:hw * (h + 1)])

    @pl.when(j < i)
    def _():
        step(False)

    @pl.when(j == i)
    def _():
        step(True)
        lam = _lambda(lq1, lk1, lq2, lk2, lam_init)
        for h in range(DA_H):
            acc = acc_scr[h]
            l = l_scr[h]
            o = acc[0:blk] / l[0:blk] - lam * (acc[blk:] / l[blk:])
            o_ref[:, hw * h:hw * (h + 1)] = (_rms(o, sg_ref[...]) * (1.0 - lam_init)).astype(o_ref.dtype)


def _flash(q, k, v, lams, subln, *, batch, blk, lam_init):
    n = q.shape[0]
    nq = n // batch // blk
    hw = 2 * DA_DH
    pairs = [(i, j) for i in range(nq) for j in range(i + 1)]
    qi = jnp.array([ij[0] for ij in pairs], jnp.int32)
    kj = jnp.array([ij[1] for ij in pairs], jnp.int32)
    qspec = pl.BlockSpec((blk, DA_W), lambda b, s, qi, kj: (b * nq + qi[s], 0))
    kspec = pl.BlockSpec((blk, DA_W), lambda b, s, qi, kj: (b * nq + kj[s], 0))
    lspec = pl.BlockSpec((1, DA_DH), lambda b, s, qi, kj: (0, 0))
    return pl.pallas_call(
        functools.partial(_flash_kernel, blk=blk, lam_init=lam_init),
        grid_spec=pltpu.PrefetchScalarGridSpec(
            num_scalar_prefetch=2, grid=(batch, len(pairs)),
            in_specs=[qspec, kspec, kspec, lspec, lspec, lspec, lspec,
                      pl.BlockSpec((1, hw), lambda b, s, qi, kj: (0, 0))],
            out_specs=qspec,
            scratch_shapes=[pltpu.VMEM((DA_H, 2 * blk, hw), BF16), pltpu.VMEM((DA_H, 2 * blk, LANES), F32),
                            pltpu.VMEM((DA_H, 2 * blk, LANES), F32), pltpu.VMEM((DA_H, 2 * blk, hw), F32),
                            pltpu.VMEM((DA_H, 2 * blk, blk), F32), pltpu.VMEM((DA_H, 2 * blk, blk), BF16),
                            pltpu.VMEM((DA_H, 2 * blk, LANES), F32)]),
        out_shape=jax.ShapeDtypeStruct((n, DA_W), BF16),
        compiler_params=_cparams(("parallel", "arbitrary")), name="flash_diff_attn",
    )(qi, kj, q, k, v, *[x.reshape(1, DA_DH) for x in lams], subln.reshape(1, hw))


def _decode_kernel(pt_ref, qbd_ref, q8_ref, kn_ref, vn_ref, lq1, lk1, lq2, lk2, sg_ref, *refs,
                   group, lam_init):
    k_refs, v_refs = refs[:group], refs[group:2 * group]
    o_ref, m_scr, l_scr, acc_scr = refs[2 * group:]
    s_idx = pl.program_id(1)
    n_maps = 2 * DA_H

    @pl.when(s_idx == 0)
    def _():
        s0 = jnp.sum(q8_ref[...] * kn_ref[...], axis=-1, keepdims=True)
        m_scr[...] = jnp.broadcast_to(s0, (n_maps, LANES))
        l_scr[...] = jnp.ones((n_maps, LANES), F32)
        for h in range(DA_H):
            acc_scr[h] = jnp.broadcast_to(vn_ref[h:h + 1, :], (n_maps, 2 * DA_DH))

    q_hi, q_lo = _split(qbd_ref[...])
    a16 = jnp.concatenate([q_hi, q_lo], axis=0)

    s_list = []
    for g in range(group):
        kt = k_refs[g][...].reshape(n_maps * DA_DH, PAGE)
        k_hi, k_lo = _split(kt)
        r = _mm(a16, k_hi) + _mm(a16, k_lo)
        s_list.append(r[0:n_maps] + r[n_maps:])
    m_prev = m_scr[...][:, :1]
    m_cur = s_list[0].max(axis=-1, keepdims=True)
    for s in s_list[1:]:
        m_cur = jnp.maximum(m_cur, s.max(axis=-1, keepdims=True))
    m_new = jnp.maximum(m_prev, m_cur)
    alpha = jnp.exp(m_prev - m_new)
    l_new = alpha * l_scr[...][:, :1]
    acc = [alpha * acc_scr[h] for h in range(DA_H)]
    for g in range(group):
        p = jnp.exp(s_list[g] - m_new)
        l_new = l_new + jnp.sum(p, axis=-1, keepdims=True)
        p_hi, p_lo = _split(p)
        p16 = jnp.concatenate([p_hi, p_lo], axis=0)
        for h in range(DA_H):
            v_hi, v_lo = _split(v_refs[g][pl.ds(h, PAGE, stride=DA_H), :])
            r = _mm(p16, v_hi) + _mm(p16, v_lo)
            acc[h] = acc[h] + (r[0:n_maps] + r[n_maps:])
    m_scr[...] = jnp.broadcast_to(m_new, (n_maps, LANES))
    l_scr[...] = jnp.broadcast_to(l_new, (n_maps, LANES))
    for h in range(DA_H):
        acc_scr[h] = acc[h]

    @pl.when(s_idx == pl.num_programs(1) - 1)
    def _():
        lam = _lambda(lq1, lk1, lq2, lk2, lam_init)
        for h in range(DA_H):
            o1 = acc[h][2 * h:2 * h + 1] / l_new[2 * h:2 * h + 1]
            o2 = acc[h][2 * h + 1:2 * h + 2] / l_new[2 * h + 1:2 * h + 2]
            o_ref[h:h + 1, :] = _rms(o1 - lam * o2, sg_ref[...]) * (1.0 - lam_init)


def _decode_attn(q, k_new, v_new, kpool, vpool, page_table, layer, lams, subln, *, lam_init, group):
    s, n_pages = page_table.shape
    assert n_pages % group == 0
    hw = 2 * DA_DH
    n_maps = 2 * DA_H
    q8 = q.reshape(s, n_maps, DA_DH)
    qbd = (q8[:, :, None, :] * jnp.eye(n_maps, dtype=F32)[None, :, :, None]).reshape(s, n_maps, DA_W)
    kn8 = k_new.reshape(s, n_maps, DA_DH)
    vn4 = v_new.reshape(s, DA_H, hw)

    small = lambda shape: pl.BlockSpec((None,) + shape, lambda b, t, pt: (b, 0, 0))
    lspec = pl.BlockSpec((1, DA_DH), lambda b, t, pt: (0, 0))
    in_specs = [small((n_maps, DA_W)), small((n_maps, DA_DH)), small((n_maps, DA_DH)), small((DA_H, hw)),
                lspec, lspec, lspec, lspec, pl.BlockSpec((1, hw), lambda b, t, pt: (0, 0))]
    in_specs += [pl.BlockSpec((None, None, n_maps, DA_DH, PAGE),
                              lambda b, t, pt, g=g: (layer, pt[b, t * group + g], 0, 0, 0)) for g in range(group)]
    in_specs += [pl.BlockSpec((None, None, PAGE * DA_H, hw),
                              lambda b, t, pt, g=g: (layer, pt[b, t * group + g], 0, 0)) for g in range(group)]
    out = pl.pallas_call(
        functools.partial(_decode_kernel, group=group, lam_init=lam_init),
        grid_spec=pltpu.PrefetchScalarGridSpec(
            num_scalar_prefetch=1, grid=(s, n_pages // group), in_specs=in_specs,
            out_specs=small((DA_H, hw)),
            scratch_shapes=[pltpu.VMEM((n_maps, LANES), F32), pltpu.VMEM((n_maps, LANES), F32),
                            pltpu.VMEM((DA_H, n_maps, hw), F32)]),
        out_shape=jax.ShapeDtypeStruct((s, DA_H, hw), F32),
        compiler_params=_cparams(("parallel", "arbitrary")), name="decode_diff_attn",
    )(page_table, qbd, q8, kn8, vn4, *[x.reshape(1, DA_DH) for x in lams], subln.reshape(1, hw),
      *([kpool] * group), *([vpool] * group))
    return out.reshape(s, DA_W)


def _tri_inverse(low, eye):
    a = -low
    t = eye + a
    ph, pl_ = _split(a)
    steps = int(math.log2(low.shape[-1])) - 1
    for _ in range(steps):
        p = _mm(ph, ph, _BNN) + (_mm(pl_, ph, _BNN) + _mm(ph, pl_, _BNN))
        ph, pl_ = _split(p)
        th, tl = _split(t)
        t = t + (_mm(th, ph, _BNN) + (_mm(tl, ph, _BNN) + _mm(th, pl_, _BNN)))
    return t


def _deltanet_kernel(qkv_ref, z_ref, ba_ref, conv0_ref, s0_ref, cw_ref, alog_ref, dtb_ref, ng_ref,
                     o_ref, s_out_ref, xbuf, s_scr, *, tt, t_real, hp):
    t = pl.program_id(1)
    ns = qkv_ref.shape[0]
    rows_all = ns * tt
    c = min(DN_CHUNK, tt)

    @pl.when(t == 0)
    def _():
        xbuf[:, 0:8, :] = conv0_ref[...]
        s_scr[...] = s0_ref[...].reshape(s_scr.shape)

    @pl.when(t > 0)
    def _():
        xbuf[:, 0:8, :] = xbuf[:, tt:tt + 8, :]

    xbuf[:, 8:8 + tt, :] = qkv_ref[...]
    cw = cw_ref[...]
    acts = []
    for u in range(ns):
        acc = xbuf[u, pl.ds(8 - (CONV_W - 1), tt), :] * cw[0:1]
        for jj in range(1, CONV_W):
            acc = acc + xbuf[u, pl.ds(8 - (CONV_W - 1) + jj, tt), :] * cw[jj:jj + 1]
        acts.append(_silu(acc))
    act = jnp.concatenate(acts, axis=0)

    ba = ba_ref[...].reshape(rows_all, LANES)
    beta_all = jax.nn.sigmoid(ba)
    sp_in = ba + dtb_ref[...]
    softplus = jnp.maximum(sp_in, 0.0) + jnp.log(1.0 + jnp.exp(-jnp.abs(sp_in)))
    g_all = -jnp.exp(alog_ref[...]) * softplus
    if t_real is not None:
        ridx = t * tt + lax.rem(lax.broadcasted_iota(jnp.int32, (rows_all, 1), 0), tt)
        live = ridx < t_real
        act = jnp.where(live, act, 0.0)
        beta_all = jnp.where(live, beta_all, 0.0)
        g_all = jnp.where(live, g_all, 0.0)

    ncs = tt // c
    nc = ns * ncs
    nb = nc * DN_H
    ri = lax.broadcasted_iota(jnp.int32, (nb, c, c), 1)
    ci = lax.broadcasted_iota(jnp.int32, (nb, c, c), 2)
    causal, strict = ri >= ci, ri > ci
    eye = jnp.where(ri == ci, 1.0, 0.0).astype(F32)

    def stack(arr, off, width):
        return jnp.stack([arr[cc * c:(cc + 1) * c, off + width * h:off + width * (h + 1)]
                          for cc in range(nc) for h in range(DN_H)], axis=0)

    rt = lax.broadcasted_iota(jnp.int32, (rows_all, rows_all), 0)
    ct = lax.broadcasted_iota(jnp.int32, (rows_all, rows_all), 1)
    chunk_tri = jnp.where((rt >= ct) & (rt // c == ct // c), 1.0, 0.0).astype(BF16)
    gc_all = _dot_exact_lhs(chunk_tri, g_all)

    q = stack(act, 0, DN_DK)
    k = stack(act, DN_QK, DN_DK)
    v = stack(act, 2 * DN_QK, DN_DV)
    q = q * lax.rsqrt(jnp.sum(q * q, axis=-1, keepdims=True) + EPS) * (DN_DK ** -0.5)
    k = k * lax.rsqrt(jnp.sum(k * k, axis=-1, keepdims=True) + EPS)
    beta = stack(beta_all, 0, 1)
    gc = stack(gc_all, DN_H, 1)
    gc_t = [jnp.transpose(gc_all[cc * c:(cc + 1) * c, :]) for cc in range(nc)]
    gc_row = jnp.stack([gc_t[cc][DN_H + h:DN_H + h + 1, :] for cc in range(nc) for h in range(DN_H)], axis=0)
    decay = jnp.where(causal, jnp.exp(jnp.where(causal, gc - gc_row, 0.0)), 0.0)
    kb, vb = k * beta, v * beta
    low = jnp.where(strict, _dot(kb, k, hp, _BNT) * decay, 0.0)
    tinv = _tri_inverse(low, eye)
    eg = jnp.exp(gc)
    sol = _dot(tinv, jnp.concatenate([vb, kb * eg], axis=-1), True, _BNN)
    u, w = sol[:, :, :DN_DV], sol[:, :, DN_DV:]
    qk = _dot(q, k, hp, _BNT) * decay
    gc_last = gc[:, c - 1:c, :]
    qd = q * eg
    kd = k * jnp.exp(gc_last - gc)
    gl = jnp.exp(gc_last)
    ng = ng_ref[...]

    s = s_scr[...]
    for cc in range(ncs):
        idx = [(sq * ncs + cc) * DN_H + h for sq in range(ns) for h in range(DN_H)]
        pick = lambda x: jnp.concatenate([x[(sq * ncs + cc) * DN_H:(sq * ncs + cc + 1) * DN_H] for sq in range(ns)], axis=0)
        v_new = pick(u) - _dot(pick(w), s, hp, _BNN)
        o = _dot(pick(qd), s, hp, _BNN) + _dot(pick(qk), v_new, hp, _BNN)
        kdt = jnp.stack([jnp.transpose(kd[i]) for i in idx], axis=0)
        s = s * pick(gl) + _dot(kdt, v_new, hp, _BNN)
        for sq in range(ns):
            for h in range(DN_H):
                zz = z_ref[sq, cc * c:(cc + 1) * c, DN_DV * h:DN_DV * (h + 1)]
                o_ref[sq, cc * c:(cc + 1) * c, DN_DV * h:DN_DV * (h + 1)] = (
                    _rms(o[sq * DN_H + h], ng) * _silu(zz)).astype(o_ref.dtype)
    s_scr[...] = s

    @pl.when(t == pl.num_programs(1) - 1)
    def _():
        s_out_ref[...] = s_scr[...].reshape(s_out_ref.shape)


def _deltanet(qkv, z, ba, conv0, s0, conv_w, a_log, dt_bias, norm_g, *, batch, t_pad, t_real, tt, hp):
    n = qkv.shape[0]
    assert n == batch * t_pad and t_pad % tt == 0 and tt % min(DN_CHUNK, tt) == 0 and tt % 16 == 0
    nt = t_pad // tt
    conv0p = jnp.pad(conv0, ((0, 0), (8 - (CONV_W - 1), 0), (0, 0)))
    cwp = jnp.pad(conv_w, ((0, 8 - CONV_W), (0, 0)))
    lane_vec = lambda a: jnp.pad(a, (DN_H, LANES - 2 * DN_H)).reshape(1, LANES)
    ns = math.gcd(batch, 4)
    seq3 = lambda a: a.reshape(batch, t_pad, a.shape[-1])
    row = lambda w: pl.BlockSpec((ns, tt, w), lambda b, t: (b, t, 0))
    const = lambda shape: pl.BlockSpec(shape, lambda b, t: (0,) * len(shape))
    state = pl.BlockSpec((ns, DN_H, DN_DK, DN_DV), lambda b, t: (b, 0, 0, 0))
    a_out, s_out = pl.pallas_call(
        functools.partial(_deltanet_kernel, tt=tt, t_real=None if t_real == t_pad else t_real, hp=hp),
        grid=(batch // ns, nt),
        in_specs=[row(DN_CH), row(DN_QK), row(LANES),
                  pl.BlockSpec((ns, 8, DN_CH), lambda b, t: (b, 0, 0)), state,
                  const((8, DN_CH)), const((1, LANES)), const((1, LANES)), const((1, DN_DV))],
        out_specs=[row(DN_QK), state],
        out_shape=[jax.ShapeDtypeStruct((batch, t_pad, DN_QK), F32 if hp else BF16),
                   jax.ShapeDtypeStruct(s0.shape, F32)],
        scratch_shapes=[pltpu.VMEM((ns, tt + 8, DN_CH), F32), pltpu.VMEM((ns * DN_H, DN_DK, DN_DV), F32)],
        compiler_params=_cparams(("parallel", "arbitrary")), name="deltanet",
    )(seq3(qkv), seq3(z), seq3(ba), conv0p, s0, cwp, lane_vec(a_log), lane_vec(dt_bias), norm_g.reshape(1, DN_DV))
    return a_out.reshape(n, DN_QK), s_out


def _mix_kernel(x_ref, a_ref, b_ref, ga_ref, gb_ref, wa_ref, wb_ref, wo_ref, fn_ref, *refs, hp, n_experts, gated):
    if n_experts:
        wr_ref, x1_ref, h_ref, comb_ref = refs
    else:
        x1_ref, h_ref = refs
    a = _dot(a_ref[...], wa_ref[...], hp)
    b = _dot(b_ref[...], wb_ref[...], hp)
    gate = (lambda r: r[...].astype(F32)) if gated else (lambda r: jax.nn.sigmoid(r[...]))
    mixed = gate(ga_ref) * a + gate(gb_ref) * b
    x1 = x_ref[...] + _dot(mixed, wo_ref[...], hp)
    x1_ref[...] = x1
    h = _rms(x1, fn_ref[...])
    h_ref[...] = h.astype(h_ref.dtype)
    if n_experts:
        logits = _dot(h, wr_ref[...], hp)
        lane = lax.broadcasted_iota(jnp.int32, logits.shape, 1).astype(F32)
        lg = jnp.where(lane < n_experts, logits, -jnp.inf)
        m1 = jnp.max(lg, axis=-1, keepdims=True)
        i1 = jnp.min(jnp.where(lg == m1, lane, float(LANES)), axis=-1, keepdims=True)
        lg2 = jnp.where(lane == i1, -jnp.inf, lg)
        m2 = jnp.max(lg2, axis=-1, keepdims=True)
        i2 = jnp.min(jnp.where(lg2 == m2, lane, float(LANES)), axis=-1, keepdims=True)
        e2 = jnp.exp(m2 - m1)
        den = 1.0 + e2
        comb_ref[...] = jnp.where(lane == i1, 1.0 / den, 0.0) + jnp.where(lane == i2, e2 / den, 0.0)


def _mix(x, a_out, b_out, g_raw, wa, wb, wo, fn, w_router, *, hp, tm, h_dtype, gated):
    n, d = x.shape
    assert n % tm == 0
    n_experts = 0 if w_router is None else w_router.shape[1]
    row = lambda w, c=0: pl.BlockSpec((tm, w), lambda i, c=c: (i, c))
    const = lambda shape: pl.BlockSpec(shape, lambda i: (0, 0))
    in_specs = [row(d), row(a_out.shape[1]), row(b_out.shape[1]), row(d, 0), row(d, 1),
                const(wa.shape), const(wb.shape), const(wo.shape), const((1, d))]
    args = [x, a_out, b_out, g_raw, g_raw, wa, wb, wo, fn.reshape(1, d)]
    out_specs = [row(d), row(d)]
    out_shape = [jax.ShapeDtypeStruct((n, d), F32), jax.ShapeDtypeStruct((n, d), h_dtype)]
    if n_experts:
        in_specs.append(const((d, LANES)))
        args.append(jnp.pad(w_router, ((0, 0), (0, LANES - n_experts))))
        out_specs.append(row(LANES))
        out_shape.append(jax.ShapeDtypeStruct((n, LANES), F32))
    return pl.pallas_call(
        functools.partial(_mix_kernel, hp=hp, n_experts=n_experts, gated=gated),
        grid=(n // tm,), in_specs=in_specs, out_specs=out_specs, out_shape=out_shape,
        compiler_params=_cparams(("parallel",)), name="mix_out",
    )(*args)


def _ffn_kernel(x1_ref, h_ref, *refs, hp, use_comb):
    if use_comb:
        comb_ref, wg_ref, wu_ref, wd_ref, o_ref = refs
    else:
        wg_ref, wu_ref, wd_ref, o_ref = refs
    e, f = pl.program_id(1), pl.program_id(2)

    @pl.when((e == 0) & (f == 0))
    def _():
        o_ref[...] = x1_ref[...]

    h = h_ref[...]
    act = _silu(_dot(h, wg_ref[...], hp)) * _dot(h, wu_ref[...], hp)
    y = _dot(act, wd_ref[...], hp)
    if use_comb:
        comb = comb_ref[...]
        lane = lax.broadcasted_iota(jnp.int32, comb.shape, 1)
        y = y * jnp.sum(jnp.where(lane == e, comb, 0.0), axis=-1, keepdims=True)
    o_ref[...] += y


def _ffn(x1, h, comb, w_gu, w_down, *, hp, tm, tf):
    n, d = x1.shape
    n_e, f_dim = w_down.shape[0], w_down.shape[1]
    assert n % tm == 0 and f_dim % tf == 0
    nf = f_dim // tf
    row = lambda w: pl.BlockSpec((tm, w), lambda i, e, f: (i, 0))
    in_specs = [row(d), row(d)]
    args = [x1, h]
    if comb is not None:
        in_specs.append(row(LANES))
        args.append(comb)
    in_specs += [pl.BlockSpec((None, d, tf), lambda i, e, f: (e, 0, f)),
                 pl.BlockSpec((None, d, tf), lambda i, e, f: (e, 0, f + nf)),
                 pl.BlockSpec((None, tf, d), lambda i, e, f: (e, f, 0))]
    args += [w_gu, w_gu, w_down]
    return pl.pallas_call(
        functools.partial(_ffn_kernel, hp=hp, use_comb=comb is not None),
        grid=(n // tm, n_e, nf), in_specs=in_specs, out_specs=row(d),
        out_shape=jax.ShapeDtypeStruct((n, d), F32),
        compiler_params=_cparams(("parallel", "arbitrary", "arbitrary")), name="ffn",
    )(*args)


def _split_w_in(w):
    o = [0]
    for s in (DN_CH, DN_QK, DN_H, DN_H, DA_W, DA_W, DA_W):
        o.append(o[-1] + s)
    ba = jnp.pad(w[:, o[2]:o[4]], ((0, 0), (0, LANES - 2 * DN_H)))
    return [w[:, o[0]:o[1]], w[:, o[1]:o[2]], ba, w[:, o[4]:o[5]], w[:, o[5]:o[6]], w[:, o[6]:o[7]], w[:, o[7]:]]


def _trunk(x, pos0, conv0, ssm0, paged, p, *, hp):
    b, t, d = x.shape
    n = b * t
    depth = p['w_in'].shape[0]
    wdt = F32 if hp else BF16
    cast = lambda w: w.astype(wdt)
    x2 = x.reshape(n, d)
    tm = min(256, n)
    cos, sin = _rope_tables(pos0 + jnp.arange(t, dtype=jnp.int32))
    chunk = DN_CHUNK if t >= DN_CHUNK else -(-t // 16) * 16
    t_pad = -(-t // chunk) * chunk
    tt = min(128, t_pad)
    k_rows, v_rows, ssm_out, conv_out = [], [], [], []
    kv_rows = None
    for l in range(depth):
        lam_init = 0.8 - 0.6 * math.exp(-0.3 * l)
        ws = [cast(w) for w in _split_w_in(p['w_in'][l])]
        lams = (p['lambda_q1'][l], p['lambda_k1'][l], p['lambda_q2'][l], p['lambda_k2'][l])
        if paged is None:
            qkv_pre, z, ba, g_raw, q_s, k_bf, v_bf, k_t, v4 = _in_proj(
                x2, p['attn_norm'][l], ws, p['q_norm'][l], p['k_norm'][l], cos, sin, kv_rows,
                batch=b, tm=tm, layer=l, depth=depth)
            kv_rows = (k_t, v4)
            b_out = _flash(q_s, k_bf, v_bf, lams, p['subln'][l], batch=b, blk=min(512, t), lam_init=lam_init)
        else:
            outs = []
            for grp in (ws[0:3], ws[3:6], ws[6:7]):
                outs += _norm_proj(x2, p['attn_norm'][l], grp, hp=hp, tm=tm)
            qkv_pre, z, ba, q_raw, k_raw, v_raw, g_raw = outs
            q_s, k_rot, _, _ = _qk_prep(q_raw, k_raw, v_raw, p['q_norm'][l], p['k_norm'][l], cos, sin,
                                        hp=hp, tm=tm, q_dtype=F32)
            kpool, vpool, page_table = paged
            b_out = _decode_attn(q_s, k_rot, v_raw, kpool, vpool, page_table, l, lams, p['subln'][l],
                                 lam_init=lam_init, group=math.gcd(32, page_table.shape[1]))
            k_rows.append(k_rot.reshape(b, t, 2 * DA_H, DA_DH))
            v_rows.append(v_raw.reshape(b, t, DA_H, 2 * DA_DH))

        def pad_t(a):
            if t_pad == t:
                return a
            return jnp.pad(a.reshape(b, t, -1), ((0, 0), (0, t_pad - t), (0, 0))).reshape(b * t_pad, -1)

        a_out, s_new = _deltanet(pad_t(qkv_pre), pad_t(z), pad_t(ba), conv0[l], ssm0[l], p['conv_w'][l],
                                 p['a_log'][l], p['dt_bias'][l], p['dn_norm'][l],
                                 batch=b, t_pad=t_pad, t_real=t, tt=tt, hp=hp)
        if t_pad != t:
            a_out = a_out.reshape(b, t_pad, -1)[:, :t].reshape(n, -1)

        moe = l % 2 == 1
        mixed = _mix(x2, a_out, b_out, g_raw, cast(p['w_a_proj'][l]), cast(p['w_b_proj'][l]), cast(p['w_out'][l]),
                     p['ffn_norm'][l], p['w_router'][l // 2] if moe else None, hp=hp, tm=min(512, n), h_dtype=wdt,
                     gated=paged is None)
        if moe:
            x1, h2, comb = mixed
            x2 = _ffn(x1, h2, comb, cast(p['w_exp_gu'][l // 2]), cast(p['w_exp_down'][l // 2]),
                      hp=hp, tm=min(1024, n), tf=p['w_exp_down'].shape[2])
        else:
            x1, h2 = mixed
            x2 = _ffn(x1, h2, None, cast(p['w_ffn_gu'][l // 2][None]), cast(p['w_ffn_down'][l // 2][None]),
                      hp=hp, tm=min(1024, n), tf=p['w_ffn_down'].shape[1] // 2)

        ssm_out.append(s_new)
        tail = qkv_pre.reshape(b, t, DN_CH)[:, max(t - (CONV_W - 1), 0):]
        conv_out.append(jnp.concatenate([conv0[l], tail], axis=1)[:, -(CONV_W - 1):])
    if paged is None:
        k_t, v4 = kv_rows
        k_all = jnp.transpose(k_t.reshape(depth, b, 2 * DA_H, DA_DH, t), (0, 1, 4, 2, 3))
        v_all = v4.reshape(depth, b, t, DA_H, 2 * DA_DH)
    else:
        k_all, v_all = jnp.stack(k_rows), jnp.stack(v_rows)
    return x2.reshape(b, t, d), k_all, v_all, jnp.stack(ssm_out), jnp.stack(conv_out)


def kernel(x_prompt, x_sample, cache_k, cache_v, state_ssm, state_conv, page_table, attn_norm, w_in, conv_w,
           a_log, dt_bias, dn_norm, q_norm, k_norm, lambda_q1, lambda_k1, lambda_q2, lambda_k2, subln,
           w_a_proj, w_b_proj, w_out, ffn_norm, w_ffn_gu, w_ffn_down, w_router, w_exp_gu, w_exp_down):
    p = dict(attn_norm=attn_norm, w_in=w_in, conv_w=conv_w, a_log=a_log, dt_bias=dt_bias, dn_norm=dn_norm,
             q_norm=q_norm, k_norm=k_norm, lambda_q1=lambda_q1, lambda_k1=lambda_k1, lambda_q2=lambda_q2,
             lambda_k2=lambda_k2, subln=subln, w_a_proj=w_a_proj, w_b_proj=w_b_proj, w_out=w_out,
             ffn_norm=ffn_norm, w_ffn_gu=w_ffn_gu, w_ffn_down=w_ffn_down, w_router=w_router,
             w_exp_gu=w_exp_gu, w_exp_down=w_exp_down)
    depth = w_in.shape[0]
    bp = x_prompt.shape[0]
    assert x_sample.shape[1] == 1, "the decode path handles one new token per sequence"
    assert cache_k.shape[2] == PAGE
    conv0 = jnp.zeros((depth, bp, CONV_W - 1, DN_CH), F32)
    ssm0 = jnp.zeros((depth, bp, DN_H, DN_DK, DN_DV), F32)
    y_p, k_p, v_p, ssm_p, conv_p = _trunk(x_prompt, 0, conv0, ssm0, None, p, hp=False)

    past_len = page_table.shape[1] * PAGE
    n_pool = cache_k.shape[1]
    kpool = jnp.transpose(cache_k, (0, 1, 3, 4, 2))
    vpool = cache_v.reshape(depth, n_pool, PAGE * DA_H, 2 * DA_DH)
    y_s, k_s, v_s, ssm_s, conv_s = _trunk(x_sample, past_len, state_conv, state_ssm, (kpool, vpool, page_table), p,
                                          hp=True)
    return (y_p, y_s, k_p, v_p, ssm_p, conv_p, k_s, v_s, ssm_s, conv_s)
```
